```python
import math
import jax, jax.numpy as jnp
from jax import lax
import numpy as np

D_MODEL = 2048
BATCH = 1
SEQ = 8192
DEPTH = 4

DIFF_HEADS = 4
DIFF_QK_DIM = 64
DIFF_V_DIM = 2 * DIFF_QK_DIM
DIFF_WIDTH = DIFF_HEADS * DIFF_V_DIM
DIFF_QK_COLS = DIFF_HEADS * 2 * DIFF_QK_DIM
SWA_Q_HEADS = 8
SWA_KV_HEADS = 2
SWA_GROUP = SWA_Q_HEADS // SWA_KV_HEADS
SWA_HEAD_DIM = 64
SWA_WINDOW = 128
SWA_WIDTH = SWA_Q_HEADS * SWA_HEAD_DIM
SWA_KV_COLS = SWA_KV_HEADS * SWA_HEAD_DIM
RWKV_HEADS = 16
RWKV_HEAD_DIM = 64
RWKV_WIDTH = RWKV_HEADS * RWKV_HEAD_DIM
DECAY_LORA = 96
ICLR_LORA = 96
GATE_LORA = 256
RWKV_COLS = 3 * RWKV_WIDTH + DECAY_LORA + ICLR_LORA + GATE_LORA

MIX_WIDTH = DIFF_WIDTH + SWA_WIDTH + RWKV_WIDTH
IN_COLS = 2 * DIFF_QK_COLS + DIFF_WIDTH + SWA_WIDTH + 2 * SWA_KV_COLS + RWKV_COLS
FFN_HIDDEN = (8 * D_MODEL + 3 * 256 - 1) // (3 * 256) * 256

ALIBI_HEADS = DIFF_HEADS + SWA_Q_HEADS
Q_BLOCK = 128
NORM_EPS = 1e-5
RWKV_GN_EPS = 64e-5
NEG_INF = -1e30

kernel_name = 'hybrid_diffattn_swa_sink_rwkv7_swiglu'


def _split(z, sizes):
    out, start = [], 0
    for s in sizes:
        out.append(z[..., start:start + s])
        start += s
    return out


def rms_norm(x, g):
    xf = x.astype(jnp.float32)
    y = xf * lax.rsqrt(jnp.mean(xf * xf, axis=-1, keepdims=True) + NORM_EPS)
    return (y * g.astype(jnp.float32)).astype(x.dtype)


def alibi_slopes():
    idx = jnp.arange(1, ALIBI_HEADS + 1, dtype=jnp.float32)
    m = jnp.exp2(-8.0 * idx / ALIBI_HEADS)
    diff_idx = np.arange(2, ALIBI_HEADS, 3)
    swa_idx = np.setdiff1d(np.arange(ALIBI_HEADS), diff_idx)
    return m[diff_idx], m[swa_idx]


def token_shift(h, mu):
    prev = jnp.pad(h, ((0, 0), (1, 0), (0, 0)))[:, :-1]
    return h + (prev - h) * mu


def diff_attention(q, k, v, lam, lambda_init, subln_g, slopes):
    b, t = q.shape[:2]
    nblk = t // Q_BLOCK
    qf = q.astype(jnp.float32) * (DIFF_QK_DIM ** -0.5)
    kf = k.astype(jnp.float32)
    vf = v.astype(jnp.float32)
    q_blocks = jnp.moveaxis(qf.reshape(b, nblk, Q_BLOCK, DIFF_HEADS, 2, DIFF_QK_DIM), 1, 0)
    kpos = jnp.arange(t)

    def one_block(args):
        q_blk, blk = args
        s = jnp.einsum('bqhcd,bkhcd->bhcqk', q_blk, kf)
        qpos = blk * Q_BLOCK + jnp.arange(Q_BLOCK)
        dist = qpos[:, None] - kpos[None, :]
        s = s - slopes[None, :, None, None, None] * dist.astype(jnp.float32)
        s = jnp.where(dist >= 0, s, NEG_INF)
        p = jax.nn.softmax(s, axis=-1)
        w = p[:, :, 0] - lam * p[:, :, 1]
        return jnp.einsum('bhqk,bkhe->bqhe', w, vf)

    o = lax.map(one_block, (q_blocks, jnp.arange(nblk)))
    o = jnp.moveaxis(o, 0, 1).reshape(b, t, DIFF_HEADS, DIFF_V_DIM)
    o = o * lax.rsqrt(jnp.mean(o * o, axis=-1, keepdims=True) + NORM_EPS)
    o = o * subln_g.astype(jnp.float32) * (1.0 - lambda_init)
    return o.reshape(b, t, DIFF_WIDTH)


def sliding_window_attention(q, k, v, sinks, slopes):
    b, t = q.shape[:2]
    W = SWA_WINDOW
    nb = t // W
    qb = (q.astype(jnp.float32) * (SWA_HEAD_DIM ** -0.5)).reshape(
        b, nb, W, SWA_KV_HEADS, SWA_GROUP, SWA_HEAD_DIM)

    def band(z):
        zb = z.astype(jnp.float32).reshape(b, nb, W, SWA_KV_HEADS, SWA_HEAD_DIM)
        prev = jnp.pad(zb, ((0, 0), (1, 0), (0, 0), (0, 0), (0, 0)))[:, :-1]
        return jnp.concatenate([prev, zb], axis=2)

    kw, vw = band(k), band(v)
    s = jnp.einsum('bnqhgd,bnkhd->bnhgqk', qb, kw)
    i = jnp.arange(W)
    j = jnp.arange(2 * W)
    dist = i[:, None] + W - j[None, :]
    key_abs = jnp.arange(nb)[:, None, None] * W - W + j[None, None, :]
    valid = (dist >= 0) & (dist < W) & (key_abs >= 0)
    bias = -slopes.reshape(SWA_KV_HEADS, SWA_GROUP)[:, :, None, None] * dist.astype(jnp.float32)
    s = jnp.where(valid[None, :, None, None], s + bias, NEG_INF)
    sink = jnp.broadcast_to(
        sinks.astype(jnp.float32).reshape(SWA_KV_HEADS, SWA_GROUP)[None, None, :, :, None, None],
        s.shape[:-1] + (1,))
    p = jax.nn.softmax(jnp.concatenate([s, sink], axis=-1), axis=-1)[..., :-1]
    o = jnp.einsum('bnhgqk,bnkhd->bnqhgd', p, vw)
    return o.reshape(b, t, SWA_WIDTH)


def wkv7_scan(r, w, k, v, a, bvec):
    b, t, H, N = r.shape

    def step(S, inp):
        r_t, w_t, k_t, v_t, a_t, b_t = inp
        sa = jnp.einsum('bhvk,bhk->bhv', S, a_t)
        S = S * w_t[:, :, None, :] + sa[..., None] * b_t[:, :, None, :] + v_t[..., None] * k_t[:, :, None, :]
        return S, jnp.einsum('bhvk,bhk->bhv', S, r_t)

    xs = tuple(jnp.moveaxis(z, 1, 0) for z in (r, w, k, v, a, bvec))
    S0 = jnp.zeros((b, H, N, N), jnp.float32)
    _, y = lax.scan(step, S0, xs)
    return jnp.moveaxis(y, 0, 1)


def rwkv7_time_mix(feats, w0, w2, a0, a2, g2, k_k, k_a, r_k, ln_w, ln_b):
    b, t = feats.shape[:2]
    H, N = RWKV_HEADS, RWKV_HEAD_DIM
    f32 = jnp.float32
    r, k, v, wl, al, gl = _split(feats.astype(f32),
                                 [RWKV_WIDTH] * 3 + [DECAY_LORA, ICLR_LORA, GATE_LORA])
    logw = -jax.nn.softplus(-(w0.astype(f32) + jnp.tanh(wl) @ w2.astype(f32))) - 0.5
    decay = jnp.exp(-jnp.exp(logw))
    a = jax.nn.sigmoid(a0.astype(f32) + al @ a2.astype(f32))
    g = jax.nn.sigmoid(gl) @ g2.astype(f32)

    def hs(z):
        return z.reshape(b, t, H, N)

    kk = hs(k * k_k.astype(f32))
    kk = kk / jnp.maximum(jnp.sqrt(jnp.sum(kk * kk, axis=-1, keepdims=True)), 1e-12)
    k = k * (1.0 + (a - 1.0) * k_a.astype(f32))
    r4, k4, v4, a4 = hs(r), hs(k), hs(v), hs(a)
    y = wkv7_scan(r4, hs(decay), k4, v4, -kk, kk * a4)
    mu = jnp.mean(y, axis=-1, keepdims=True)
    var = jnp.mean(jnp.square(y - mu), axis=-1, keepdims=True)
    y = ((y - mu) * lax.rsqrt(var + RWKV_GN_EPS)).reshape(b, t, RWKV_WIDTH)
    y = y * ln_w.astype(f32) + ln_b.astype(f32)
    bonus = jnp.sum(r4 * k4 * r_k.astype(f32), axis=-1, keepdims=True) * v4
    y = y + bonus.reshape(b, t, RWKV_WIDTH)
    return y * g


def setup_inputs(seed: int = 0) -> dict:
    key = jax.random.key(seed)
    ks = jax.random.split(key, 22)
    f32 = jnp.float32
    L = DEPTH

    def nrm(k, shape, scale):
        return jax.random.normal(k, shape, f32) * scale

    return {
        'x': nrm(ks[0], (BATCH, SEQ, D_MODEL), 1.0),
        'attn_norm_g': 1.0 + nrm(ks[1], (L, D_MODEL), 0.02),
        'w_in': nrm(ks[2], (L, D_MODEL, IN_COLS), D_MODEL ** -0.5),
        'diff_lambda': nrm(ks[3], (L, 4, DIFF_QK_DIM), 0.1),
        'diff_subln_g': 1.0 + nrm(ks[4], (L, DIFF_V_DIM), 0.02),
        'swa_sinks': nrm(ks[5], (L, SWA_Q_HEADS), 0.5),
        'rwkv_mu': jax.random.uniform(ks[6], (L, RWKV_COLS), f32),
        'rwkv_w0': jax.random.uniform(ks[7], (L, RWKV_WIDTH), f32, minval=-5.0, maxval=0.0),
        'rwkv_w2': nrm(ks[8], (L, DECAY_LORA, RWKV_WIDTH), 0.3 * DECAY_LORA ** -0.5),
        'rwkv_a0': nrm(ks[9], (L, RWKV_WIDTH), 0.2),
        'rwkv_a2': nrm(ks[10], (L, ICLR_LORA, RWKV_WIDTH), 0.3 * ICLR_LORA ** -0.5),
        'rwkv_g2': nrm(ks[11], (L, GATE_LORA, RWKV_WIDTH), GATE_LORA ** -0.5),
        'rwkv_k_k': 0.85 + nrm(ks[12], (L, RWKV_WIDTH), 0.02),
        'rwkv_k_a': 1.0 + nrm(ks[13], (L, RWKV_WIDTH), 0.02),
        'rwkv_r_k': nrm(ks[14], (L, RWKV_HEADS, RWKV_HEAD_DIM), 0.1),
        'rwkv_ln_w': 1.0 + nrm(ks[15], (L, RWKV_WIDTH), 0.02),
        'rwkv_ln_b': nrm(ks[16], (L, RWKV_WIDTH), 0.02),
        'w_out': nrm(ks[17], (L, MIX_WIDTH, D_MODEL), MIX_WIDTH ** -0.5),
        'ffn_norm_g': 1.0 + nrm(ks[18], (L, D_MODEL), 0.02),
        'w_gate_up': nrm(ks[19], (L, D_MODEL, 2 * FFN_HIDDEN), D_MODEL ** -0.5),
        'w_down': nrm(ks[20], (L, FFN_HIDDEN, D_MODEL), FFN_HIDDEN ** -0.5),
        'final_norm_g': 1.0 + nrm(ks[21], (D_MODEL,), 0.02),
    }


def reference(x, attn_norm_g, w_in, diff_lambda, diff_subln_g, swa_sinks, rwkv_mu,
              rwkv_w0, rwkv_w2, rwkv_a0, rwkv_a2, rwkv_g2, rwkv_k_k, rwkv_k_a, rwkv_r_k,
              rwkv_ln_w, rwkv_ln_b, w_out, ffn_norm_g, w_gate_up, w_down, final_norm_g):
    b, t, _ = x.shape
    diff_slopes, swa_slopes = alibi_slopes()
    for l in range(DEPTH):
        h = rms_norm(x, attn_norm_g[l])
        proj = h @ w_in[l]
        qa, ka, va, qb, kb, vb, rw = _split(
            proj, [DIFF_QK_COLS, DIFF_QK_COLS, DIFF_WIDTH, SWA_WIDTH, SWA_KV_COLS, SWA_KV_COLS, RWKV_COLS])

        lambda_init = 0.8 - 0.6 * math.exp(-0.3 * l)
        lamv = diff_lambda[l].astype(jnp.float32)
        lam = jnp.exp(jnp.sum(lamv[0] * lamv[1])) - jnp.exp(jnp.sum(lamv[2] * lamv[3])) + lambda_init
        ya = diff_attention(qa.reshape(b, t, DIFF_HEADS, 2, DIFF_QK_DIM),
                            ka.reshape(b, t, DIFF_HEADS, 2, DIFF_QK_DIM),
                            va.reshape(b, t, DIFF_HEADS, DIFF_V_DIM),
                            lam, lambda_init, diff_subln_g[l], diff_slopes)

        yb = sliding_window_attention(qb.reshape(b, t, SWA_Q_HEADS, SWA_HEAD_DIM),
                                      kb.reshape(b, t, SWA_KV_HEADS, SWA_HEAD_DIM),
                                      vb.reshape(b, t, SWA_KV_HEADS, SWA_HEAD_DIM),
                                      swa_sinks[l], swa_slopes)

        feats = token_shift(rw, rwkv_mu[l])
        yc = rwkv7_time_mix(feats, rwkv_w0[l], rwkv_w2[l], rwkv_a0[l], rwkv_a2[l], rwkv_g2[l],
                            rwkv_k_k[l], rwkv_k_a[l], rwkv_r_k[l], rwkv_ln_w[l], rwkv_ln_b[l])

        mix = jnp.concatenate([ya.astype(x.dtype), yb.astype(x.dtype), yc.astype(x.dtype)], axis=-1)
        x = x + mix @ w_out[l]

        h = rms_norm(x, ffn_norm_g[l])
        gate, up = _split(h @ w_gate_up[l], [FFN_HIDDEN, FFN_HIDDEN])
        x = x + (jax.nn.silu(gate) * up) @ w_down[l]
    return rms_norm(x, final_norm_g)
```

```python
import functools
import math

import numpy as np
import jax
import jax.numpy as jnp
from jax import lax
from jax.experimental import pallas as pl
from jax.experimental.pallas import tpu as pltpu

F32 = jnp.float32
BF16 = jnp.bfloat16

D_MODEL = 2048
LANES = 128
DIFF_HEADS = 4
DIFF_QK_DIM = 64
SWA_Q_HEADS = 8
SWA_HEAD_DIM = 64
SWA_WINDOW = 128
RWKV_HEADS = 16
RWKV_HEAD_DIM = 64
RWKV_WIDTH = RWKV_HEADS * RWKV_HEAD_DIM
RWKV_PAIRS = RWKV_WIDTH // LANES
DECAY_LORA = 96
ICLR_LORA = 96
GATE_LORA = 256
FFN_HIDDEN = 5632
ALIBI_HEADS = 12
NORM_EPS = 1e-5
RWKV_GN_EPS = 64e-5
NEG_INF = -1e30
CHUNK = 64

PROJ_COLS = 6144
BLK_QA, BLK_KA, BLK_VA = 0, 4, 8
BLK_QB = 12
BLK_KB, BLK_VB = 16, 18
BLK_R, BLK_K, BLK_V = 20, 28, 36
BLK_LORA = 44

VMEM_LIMIT = 56 * 1024 * 1024


def _cparams(sem):
    return pltpu.CompilerParams(dimension_semantics=sem, vmem_limit_bytes=VMEM_LIMIT)


def _dot(a, b):
    return jnp.dot(a, b, preferred_element_type=F32)


def _dot_nt(a, b):
    return lax.dot_general(a, b, (((1,), (1,)), ((), ())), preferred_element_type=F32)


def _norm_to_scratch(x_ref, g_ref, h_ref):
    x = x_ref[...]
    ms = jnp.mean(x * x, axis=-1, keepdims=True)
    h_ref[...] = (x * lax.rsqrt(ms + NORM_EPS) * g_ref[...]).astype(BF16)


def _norm_mm_kernel(x_ref, g_ref, w_ref, o_ref, h_ref):
    @pl.when(pl.program_id(1) == 0)
    def _():
        _norm_to_scratch(x_ref, g_ref, h_ref)

    o_ref[...] = _dot(h_ref[...], w_ref[...])


def norm_matmul(x, g, w, *, tm=1024, tn=512):
    t, d = x.shape
    n = w.shape[1]
    return pl.pallas_call(
        _norm_mm_kernel,
        out_shape=jax.ShapeDtypeStruct((t, n), F32),
        grid=(t // tm, n // tn),
        in_specs=[
            pl.BlockSpec((tm, d), lambda i, j: (i, 0)),
            pl.BlockSpec((1, d), lambda i, j: (0, 0)),
            pl.BlockSpec((d, tn), lambda i, j: (0, j)),
        ],
        out_specs=pl.BlockSpec((tm, tn), lambda i, j: (i, j)),
        scratch_shapes=[pltpu.VMEM((tm, d), BF16)],
        compiler_params=_cparams(("parallel", "arbitrary")),
        name="norm_in_proj",
    )(x, g.reshape(1, d), w)


def _ffn_up_kernel(x_ref, g_ref, wg_ref, wu_ref, o_ref, h_ref):
    @pl.when(pl.program_id(1) == 0)
    def _():
        _norm_to_scratch(x_ref, g_ref, h_ref)

    h = h_ref[...]
    gate = _dot(h, wg_ref[...])
    up = _dot(h, wu_ref[...])
    o_ref[...] = (gate / (1.0 + jnp.exp(-gate)) * up).astype(BF16)


def ffn_up(x, g, w_gate_up, *, tm=1024, tn=512):
    t, d = x.shape
    hid = w_gate_up.shape[1] // 2
    nj = hid // tn
    return pl.pallas_call(
        _ffn_up_kernel,
        out_shape=jax.ShapeDtypeStruct((t, hid), BF16),
        grid=(t // tm, nj),
        in_specs=[
            pl.BlockSpec((tm, d), lambda i, j: (i, 0)),
            pl.BlockSpec((1, d), lambda i, j: (0, 0)),
            pl.BlockSpec((d, tn), lambda i, j: (0, j)),
            pl.BlockSpec((d, tn), lambda i, j: (0, j + nj)),
        ],
        out_specs=pl.BlockSpec((tm, tn), lambda i, j: (i, j)),
        scratch_shapes=[pltpu.VMEM((tm, d), BF16)],
        compiler_params=_cparams(("parallel", "arbitrary")),
        name="ffn_up",
    )(x, g.reshape(1, d), w_gate_up, w_gate_up)


def _mm_res_kernel(*refs, splits):
    n = len(splits)
    a_refs, (w_ref, x_ref, o_ref) = refs[:n], refs[n:]
    acc = x_ref[...]
    off = 0
    for a_ref, k in zip(a_refs, splits):
        acc = acc + _dot(a_ref[...], w_ref[off:off + k, :])
        off += k
    o_ref[...] = acc


def matmul_residual(a_list, w, x, *, tm=1024, tn=512, name):
    t, n = x.shape
    splits = tuple(a.shape[1] for a in a_list)
    k = sum(splits)
    in_specs = [pl.BlockSpec((tm, s), lambda i, j: (i, 0)) for s in splits]
    in_specs += [
        pl.BlockSpec((k, tn), lambda i, j: (0, j)),
        pl.BlockSpec((tm, tn), lambda i, j: (i, j)),
    ]
    return pl.pallas_call(
        functools.partial(_mm_res_kernel, splits=splits),
        out_shape=jax.ShapeDtypeStruct((t, n), F32),
        grid=(t // tm, n // tn),
        in_specs=in_specs,
        out_specs=pl.BlockSpec((tm, tn), lambda i, j: (i, j)),
        compiler_params=_cparams(("parallel", "arbitrary")),
        name=name,
    )(*a_list, w, x)


def _rmsnorm_kernel(x_ref, g_ref, o_ref):
    x = x_ref[...]
    ms = jnp.mean(x * x, axis=-1, keepdims=True)
    o_ref[...] = x * lax.rsqrt(ms + NORM_EPS) * g_ref[...]


def rmsnorm(x, g, *, tm=512):
    t, d = x.shape
    return pl.pallas_call(
        _rmsnorm_kernel,
        out_shape=jax.ShapeDtypeStruct((t, d), F32),
        grid=(t // tm,),
        in_specs=[pl.BlockSpec((tm, d), lambda i: (i, 0)), pl.BlockSpec((1, d), lambda i: (0, 0))],
        out_specs=pl.BlockSpec((tm, d), lambda i: (i, 0)),
        compiler_params=_cparams(("parallel",)),
        name="final_rmsnorm",
    )(x, g.reshape(1, d))


def _diff_attn_kernel(qi_ref, kj_ref, slope_ref, q_ref, k_ref, v_ref, lam_ref, g_ref, o_ref,
                      qs_ref, m_ref, l_ref, acc_ref, *, tq, tk, lambda_init):
    h = pl.program_id(0)
    t = pl.program_id(1)
    qi = qi_ref[t]
    kj = kj_ref[t]

    @pl.when(kj == 0)
    def _():
        q = q_ref[...] * (DIFF_QK_DIM ** -0.5)
        lane = lax.broadcasted_iota(jnp.int32, q.shape, 1)
        qs_ref[0] = jnp.where(lane < DIFF_QK_DIM, q, 0.0).astype(BF16)
        qs_ref[1] = jnp.where(lane < DIFF_QK_DIM, 0.0, q).astype(BF16)
        m_ref[...] = jnp.full(m_ref.shape, NEG_INF, F32)
        l_ref[...] = jnp.zeros(l_ref.shape, F32)
        acc_ref[...] = jnp.zeros(acc_ref.shape, F32)

    k = k_ref[...].astype(BF16)
    v = v_ref[...].astype(BF16)
    row = lax.broadcasted_iota(jnp.int32, (tq, tk), 0)
    col = lax.broadcasted_iota(jnp.int32, (tq, tk), 1)
    dist = (qi * tq - kj * tk) + row - col
    bias = slope_ref[h] * dist.astype(F32)
    for c in range(2):
        s = _dot_nt(qs_ref[c], k)
        s = jnp.where(dist >= 0, s - bias, NEG_INF)
        m_prev = m_ref[c]
        m_new = jnp.maximum(m_prev, jnp.max(s, axis=-1, keepdims=True))
        alpha = jnp.exp(m_prev - m_new)
        p = jnp.exp(s - m_new)
        l_ref[c] = alpha * l_ref[c] + jnp.sum(p, axis=-1, keepdims=True)
        acc_ref[c] = alpha * acc_ref[c] + _dot(p.astype(BF16), v)
        m_ref[c] = m_new

    @pl.when(kj == qi)
    def _():
        lamv = lam_ref[...]
        s01 = jnp.sum(lamv[0:1] * lamv[1:2], axis=-1, keepdims=True)
        s23 = jnp.sum(lamv[2:3] * lamv[3:4], axis=-1, keepdims=True)
        lam = jnp.exp(s01) - jnp.exp(s23) + lambda_init
        o = acc_ref[0] / l_ref[0] - lam * (acc_ref[1] / l_ref[1])
        o = o * lax.rsqrt(jnp.mean(o * o, axis=-1, keepdims=True) + NORM_EPS)
        o_ref[...] = (o * g_ref[...] * (1.0 - lambda_init)).astype(BF16)


def diff_attention(proj, lamv, subln_g, slopes, *, lambda_init, tq=512):
    t = proj.shape[0]
    tk = tq
    nq = t // tq
    qi = np.concatenate([np.full(i + 1, i, np.int32) for i in range(nq)])
    kj = np.concatenate([np.arange(i + 1, dtype=np.int32) for i in range(nq)])
    grid_spec = pltpu.PrefetchScalarGridSpec(
        num_scalar_prefetch=2,
        grid=(DIFF_HEADS, len(qi)),
        in_specs=[
            pl.BlockSpec(memory_space=pltpu.SMEM),
            pl.BlockSpec((tq, LANES), lambda h, s, qi, kj: (qi[s], BLK_QA + h)),
            pl.BlockSpec((tk, LANES), lambda h, s, qi, kj: (kj[s], BLK_KA + h)),
            pl.BlockSpec((tk, LANES), lambda h, s, qi, kj: (kj[s], BLK_VA + h)),
            pl.BlockSpec((4, DIFF_QK_DIM), lambda h, s, qi, kj: (0, 0)),
            pl.BlockSpec((1, LANES), lambda h, s, qi, kj: (0, 0)),
        ],
        out_specs=pl.BlockSpec((tq, LANES), lambda h, s, qi, kj: (qi[s], h)),
        scratch_shapes=[
            pltpu.VMEM((2, tq, LANES), BF16),
            pltpu.VMEM((2, tq, 1), F32),
            pltpu.VMEM((2, tq, 1), F32),
            pltpu.VMEM((2, tq, LANES), F32),
        ],
    )
    return pl.pallas_call(
        functools.partial(_diff_attn_kernel, tq=tq, tk=tk, lambda_init=lambda_init),
        out_shape=jax.ShapeDtypeStruct((t, DIFF_HEADS * LANES), BF16),
        grid_spec=grid_spec,
        compiler_params=_cparams(("parallel", "arbitrary")),
        name="diff_attention",
    )(jnp.asarray(qi), jnp.asarray(kj), slopes, proj, proj, proj, lamv, subln_g.reshape(1, LANES))


def _swa_kernel(sink_ref, q_ref, k_ref, v_ref, kh_ref, vh_ref, o_ref, *, tq, slopes):
    i = pl.program_id(0)
    w = SWA_WINDOW
    lane = lax.broadcasted_iota(jnp.int32, (w, LANES), 1)
    low = lane < SWA_HEAD_DIM
    row = lax.broadcasted_iota(jnp.int32, (w, 2 * w), 0)
    col = lax.broadcasted_iota(jnp.int32, (w, 2 * w), 1)
    dist = row + w - col
    valid = jnp.logical_and(dist >= 0, dist < w)
    valid0 = jnp.logical_and(valid, jnp.logical_or(col >= w, i > 0))
    distf = dist.astype(F32)
    for r in range(tq // w):
        rs = slice(r * w, (r + 1) * w)
        ok = valid0 if r == 0 else valid
        for kv in range(2):
            ks = slice(kv * LANES, (kv + 1) * LANES)
            if r == 0:
                kband = jnp.concatenate([kh_ref[:, ks], k_ref[0:w, ks]], axis=0)
                vband = jnp.concatenate([vh_ref[:, ks], v_ref[0:w, ks]], axis=0)
            else:
                kband = k_ref[(r - 1) * w:(r + 1) * w, ks]
                vband = v_ref[(r - 1) * w:(r + 1) * w, ks]
            kband = kband.astype(BF16)
            vband = vband.astype(BF16)
            for j in range(2):
                jj = kv * 2 + j
                q = q_ref[rs, jj * LANES:(jj + 1) * LANES] * (SWA_HEAD_DIM ** -0.5)
                outs = []
                for half in range(2):
                    hq = 2 * jj + half
                    qh = jnp.where(low, q, 0.0) if half == 0 else jnp.where(low, 0.0, q)
                    s = _dot_nt(qh.astype(BF16), kband)
                    s = jnp.where(ok, s - slopes[hq] * distf, NEG_INF)
                    sink = sink_ref[hq]
                    m = jnp.maximum(jnp.max(s, axis=-1, keepdims=True), sink)
                    e = jnp.exp(s - m)
                    den = jnp.sum(e, axis=-1, keepdims=True) + jnp.exp(sink - m)
                    outs.append(_dot((e / den).astype(BF16), vband))
                o_ref[rs, jj * LANES:(jj + 1) * LANES] = jnp.where(low, outs[0], outs[1]).astype(BF16)


def sliding_window_attention(proj, sinks, slopes, *, tq=512):
    t = proj.shape[0]
    w = SWA_WINDOW
    qw = SWA_Q_HEADS * SWA_HEAD_DIM
    kw = 2 * LANES
    rpb = tq // w
    return pl.pallas_call(
        functools.partial(_swa_kernel, tq=tq, slopes=tuple(float(s) for s in slopes)),
        out_shape=jax.ShapeDtypeStruct((t, qw), BF16),
        grid=(t // tq,),
        in_specs=[
            pl.BlockSpec(memory_space=pltpu.SMEM),
            pl.BlockSpec((tq, qw), lambda i: (i, BLK_QB * LANES // qw)),
            pl.BlockSpec((tq, kw), lambda i: (i, BLK_KB * LANES // kw)),
            pl.BlockSpec((tq, kw), lambda i: (i, BLK_VB * LANES // kw)),
            pl.BlockSpec((w, kw), lambda i: (jnp.maximum(i * rpb - 1, 0), BLK_KB * LANES // kw)),
            pl.BlockSpec((w, kw), lambda i: (jnp.maximum(i * rpb - 1, 0), BLK_VB * LANES // kw)),
        ],
        out_specs=pl.BlockSpec((tq, qw), lambda i: (i, 0)),
        compiler_params=_cparams(("parallel",)),
        name="swa_attention",
    )(sinks, proj, proj, proj, proj, proj)


def _seg_sum(x, low):
    s0 = jnp.sum(jnp.where(low, x, 0.0), axis=-1, keepdims=True)
    s1 = jnp.sum(jnp.where(low, 0.0, x), axis=-1, keepdims=True)
    return jnp.where(low, s0, s1)


def _split3(x):
    hi = x.astype(BF16)
    r1 = x - hi.astype(F32)
    mid = r1.astype(BF16)
    lo = (r1 - mid.astype(F32)).astype(BF16)
    return hi, mid, lo


def _stack_heads(z, low):
    return jnp.concatenate([jnp.where(low, z, 0.0), jnp.where(low, 0.0, z)], axis=0)


def _rwkv_kernel(r_ref, k_ref, v_ref, lo_ref, rh_ref, kh_ref, vh_ref, loh_ref, pp_ref, mul_ref,
                 w2_ref, a2_ref, g2_ref, y_ref,
                 h_ref, at_ref, rt_ref, bt_ref, kt_ref, bh_ref, khat_ref, vs_ref, et_ref,
                 wr_ref, u0_ref, y0_ref, arb_ref, sbt_ref, kv_ref, dx_ref, ys_ref, *, tt):
    i = pl.program_id(1)
    nch = tt // CHUNK
    c2 = 2 * CHUNK

    @pl.when(i == 0)
    def _():
        h_ref[...] = jnp.zeros(h_ref.shape, F32)

    lane = lax.broadcasted_iota(jnp.int32, (tt, LANES), 1)
    low = lane < RWKV_HEAD_DIM

    def shift(x_ref, halo_ref, mu):
        x = x_ref[...]
        rowi = lax.broadcasted_iota(jnp.int32, x.shape, 0)
        first = jnp.where(i > 0, halo_ref[7:8, :], 0.0)
        prev = jnp.where(rowi == 0, first, pltpu.roll(x, 1, axis=0))
        return x + (prev - x) * mu

    pp = pp_ref[...]
    mu_r, mu_k, mu_v = pp[0:1], pp[1:2], pp[2:3]
    w0, a0, k_k, k_a, r_k, ln_w, ln_b = (pp[3:4], pp[4:5], pp[5:6], pp[6:7], pp[7:8], pp[8:9],
                                         pp[9:10])
    r = shift(r_ref, rh_ref, mu_r)
    k = shift(k_ref, kh_ref, mu_k)
    v = shift(v_ref, vh_ref, mu_v)
    lo = shift(lo_ref, loh_ref, mul_ref[...])
    wl, al, gl = lo[:, 0:LANES], lo[:, LANES:2 * LANES], lo[:, 2 * LANES:4 * LANES]

    z = -(w0 + _dot(jnp.tanh(wl).astype(BF16), w2_ref[...]))
    softplus = jnp.maximum(z, 0.0) + jnp.log(1.0 + jnp.exp(-jnp.abs(z)))
    lw = -jnp.exp(-softplus - 0.5)
    a = 1.0 / (1.0 + jnp.exp(-(a0 + _dot(al.astype(BF16), a2_ref[...]))))
    g = _dot((1.0 / (1.0 + jnp.exp(-gl))).astype(BF16), g2_ref[...])

    kx = k * k_k
    kk = kx / jnp.maximum(jnp.sqrt(_seg_sum(kx * kx, low)), 1e-12)
    k2 = k * (1.0 + (a - 1.0) * k_a)
    bvec = kk * a

    ri = lax.broadcasted_iota(jnp.int32, (tt, tt), 0)
    ci = lax.broadcasted_iota(jnp.int32, (tt, tt), 1)
    same = (ri // CHUNK) == (ci // CHUNK)
    ltri = jnp.where(jnp.logical_and(same, ri >= ci), 1.0, 0.0).astype(BF16)
    lall = jnp.where(same, 1.0, 0.0).astype(BF16)
    parts = _split3(lw)
    cum = _dot(ltri, parts[0]) + _dot(ltri, parts[1]) + _dot(ltri, parts[2])
    tot = _dot(lall, parts[0]) + _dot(lall, parts[1]) + _dot(lall, parts[2])

    einv = jnp.exp(-cum)
    erem = jnp.exp(tot - cum)
    at_ref[...] = (-kk * jnp.exp(cum - lw)).astype(BF16)
    rt_ref[...] = (r * jnp.exp(cum)).astype(BF16)
    bt_ref[...] = (bvec * einv).astype(BF16)
    kt_ref[...] = (k2 * einv).astype(BF16)
    bh_ref[...] = bvec * erem
    khat_ref[...] = k2 * erem
    vs_ref[...] = v.astype(BF16)
    et_ref[...] = jnp.exp(tot)

    low_c = lax.broadcasted_iota(jnp.int32, (CHUNK, LANES), 1) < RWKV_HEAD_DIM
    rr =lax.broadcasted_iota(jnp.int32, (c2, c2), 0)
    cc = lax.broadcasted_iota(jnp.int32, (c2, c2), 1)
    lt = rr > cc
    le = rr >= cc
    eye = jnp.where(rr == cc, 1.0, 0.0)

    def prepare(c, carry):
        sl = pl.ds(pl.multiple_of(c * CHUNK, CHUNK), CHUNK)
        s2 = pl.ds(pl.multiple_of(c * c2, c2), c2)
        s4 = pl.ds(pl.multiple_of(c * 2 * c2, 2 * c2), 2 * c2)
        sa = _stack_heads(at_ref[sl, :], low_c)
        sr = _stack_heads(rt_ref[sl, :], low_c)
        sb = _stack_heads(bt_ref[sl, :], low_c)
        sk = _stack_heads(kt_ref[sl, :], low_c)
        sv = _stack_heads(vs_ref[sl, :], low_c)
        a4 = _dot_nt(jnp.concatenate([sa, sr], axis=0), jnp.concatenate([sb, sk], axis=0))
        n = jnp.where(lt, a4[:c2, :c2], 0.0)
        aak = jnp.where(lt, a4[:c2, c2:], 0.0)
        arb = jnp.where(le, a4[c2:, :c2], 0.0)
        ark = jnp.where(le, a4[c2:, c2:], 0.0)
        tinv = eye + n
        p = n
        for _ in range(int(math.log2(CHUNK)) - 1):
            pb = p.astype(BF16)
            p = _dot(pb, pb)
            tinv = tinv + _dot(p.astype(BF16), tinv.astype(BF16))
        av = _dot(jnp.concatenate([aak, ark], axis=0).astype(BF16), sv)
        wu = _dot(tinv.astype(BF16), jnp.concatenate([sa, av[:c2].astype(BF16)], axis=1))
        wr_ref[s4, :] = jnp.concatenate([wu[:, :LANES].astype(BF16), sr], axis=0)
        u0_ref[s2, :] = wu[:, LANES:]
        y0_ref[s2, :] = av[c2:]
        arb_ref[s2, :] = arb.astype(BF16)
        sbt_ref[s2, :] = _stack_heads(bh_ref[sl, :], low_c).T.astype(BF16)
        kv_ref[s2, :] = _dot(_stack_heads(khat_ref[sl, :], low_c).T.astype(BF16), sv)
        dx_ref[s2, :] = jnp.broadcast_to(et_ref[sl, :][0:1, :], (c2, LANES)).T
        return carry

    lax.fori_loop(0, nch, prepare, 0)

    def advance(c, h):
        sl = pl.ds(pl.multiple_of(c * CHUNK, CHUNK), CHUNK)
        s2 = pl.ds(pl.multiple_of(c * c2, c2), c2)
        s4 = pl.ds(pl.multiple_of(c * 2 * c2, 2 * c2), 2 * c2)
        wh = _dot(wr_ref[s4, :], h.astype(BF16))
        u = (wh[:c2] + u0_ref[s2, :]).astype(BF16)
        ys = wh[c2:] + _dot(arb_ref[s2, :], u) + y0_ref[s2, :]
        ys_ref[sl, :] = ys[:CHUNK] + ys[CHUNK:]
        return dx_ref[s2, :] * h + _dot(sbt_ref[s2, :], u) + kv_ref[s2, :]

    h_ref[...] = lax.fori_loop(0, nch, advance, h_ref[...])

    y = ys_ref[...]
    mean = _seg_sum(y, low) * (1.0 / RWKV_HEAD_DIM)
    yc = y - mean
    var = _seg_sum(yc * yc, low) * (1.0 / RWKV_HEAD_DIM)
    yn = yc * lax.rsqrt(var + RWKV_GN_EPS) * ln_w + ln_b
    bonus = _seg_sum(r * k2 * r_k, low) * v
    y_ref[...] = ((yn + bonus) * g).astype(BF16)


def rwkv7_mix(proj, pp, mu_lora, w2p, a2p, g2, *, tt=512):
    t = proj.shape[0]
    nch = tt // CHUNK
    c2 = 2 * CHUNK
    lw = 4 * LANES
    hb = tt // 8
    halo = lambda blk: (lambda p, i: (jnp.maximum(i * hb - 1, 0), blk + p))
    in_specs = [
        pl.BlockSpec((tt, LANES), lambda p, i: (i, BLK_R + p)),
        pl.BlockSpec((tt, LANES), lambda p, i: (i, BLK_K + p)),
        pl.BlockSpec((tt, LANES), lambda p, i: (i, BLK_V + p)),
        pl.BlockSpec((tt, lw), lambda p, i: (i, BLK_LORA * LANES // lw)),
        pl.BlockSpec((8, LANES), halo(BLK_R)),
        pl.BlockSpec((8, LANES), halo(BLK_K)),
        pl.BlockSpec((8, LANES), halo(BLK_V)),
        pl.BlockSpec((8, lw), lambda p, i: (jnp.maximum(i * hb - 1, 0), BLK_LORA * LANES // lw)),
        pl.BlockSpec((16, LANES), lambda p, i: (0, p)),
        pl.BlockSpec((1, lw), lambda p, i: (0, 0)),
        pl.BlockSpec((LANES, LANES), lambda p, i: (0, p)),
        pl.BlockSpec((LANES, LANES), lambda p, i: (0, p)),
        pl.BlockSpec((GATE_LORA, LANES), lambda p, i: (0, p)),
    ]
    scratch = [
        pltpu.VMEM((c2, LANES), F32),
        pltpu.VMEM((tt, LANES), BF16),
        pltpu.VMEM((tt, LANES), BF16),
        pltpu.VMEM((tt, LANES), BF16),
        pltpu.VMEM((tt, LANES), BF16),
        pltpu.VMEM((tt, LANES), F32),
        pltpu.VMEM((tt, LANES), F32),
        pltpu.VMEM((tt, LANES), BF16),
        pltpu.VMEM((tt, LANES), F32),
        pltpu.VMEM((nch * 2 * c2, LANES), BF16),
        pltpu.VMEM((nch * c2, LANES), F32),
        pltpu.VMEM((nch * c2, LANES), F32),
        pltpu.VMEM((nch * c2, LANES), BF16),
        pltpu.VMEM((nch * c2, LANES), BF16),
        pltpu.VMEM((nch * c2, LANES), F32),
        pltpu.VMEM((nch * c2, LANES), F32),
        pltpu.VMEM((tt, LANES), F32),
    ]
    return pl.pallas_call(
        functools.partial(_rwkv_kernel, tt=tt),
        out_shape=jax.ShapeDtypeStruct((t, RWKV_WIDTH), BF16),
        grid=(RWKV_PAIRS, t // tt),
        in_specs=in_specs,
        out_specs=pl.BlockSpec((tt, LANES), lambda p, i: (i, p)),
        scratch_shapes=scratch,
        compiler_params=_cparams(("parallel", "arbitrary")),
        name="rwkv7_mix",
    )(proj, proj, proj, proj, proj, proj, proj, proj, pp, mu_lora, w2p, a2p, g2)


def _alibi_slopes():
    idx = np.arange(1, ALIBI_HEADS + 1, dtype=np.float32)
    m = np.exp2(-8.0 * idx / ALIBI_HEADS).astype(np.float32)
    diff_idx = np.arange(2, ALIBI_HEADS, 3)
    swa_idx = np.setdiff1d(np.arange(ALIBI_HEADS), diff_idx)
    return m[diff_idx], m[swa_idx]


def _pad_cols(w, width):
    return jnp.pad(w, [(0, 0)] * (w.ndim - 1) + [(0, width - w.shape[-1])])


def _reorder_in_proj(w):
    c = lambda a, b: w[..., a:b]
    kb0, kb1 = c(2048, 2112), c(2112, 2176)
    vb0, vb1 = c(2176, 2240), c(2240, 2304)
    return jnp.concatenate([
        c(0, 2048), kb0, kb0, kb1, kb1, vb0, vb0, vb1, vb1, c(2304, 5376),
        _pad_cols(c(5376, 5472), LANES), _pad_cols(c(5472, 5568), LANES), c(5568, 5824)], axis=-1)


def kernel(x, attn_norm_g, w_in, diff_lambda, diff_subln_g, swa_sinks, rwkv_mu, rwkv_w0, rwkv_w2,
           rwkv_a0, rwkv_a2, rwkv_g2, rwkv_k_k, rwkv_k_a, rwkv_r_k, rwkv_ln_w, rwkv_ln_b, w_out,
           ffn_norm_g, w_gate_up, w_down, final_norm_g):
    b, t, d = x.shape
    depth = w_in.shape[0]
    diff_slopes, swa_slopes = _alibi_slopes()
    diff_slopes = jnp.asarray(diff_slopes)

    w_in_b = _reorder_in_proj(w_in).astype(BF16)
    w_out_b = w_out.astype(BF16)
    w_gu_b = w_gate_up.astype(BF16)
    w_down_b = w_down.astype(BF16)
    pad_rows = lambda w: jnp.pad(w, ((0, 0), (0, LANES - w.shape[1]), (0, 0)))
    w2_b = pad_rows(rwkv_w2).astype(BF16)
    a2_b = pad_rows(rwkv_a2).astype(BF16)
    g2_b = rwkv_g2.astype(BF16)
    mu = rwkv_mu
    mu_lora = jnp.concatenate([
        _pad_cols(mu[:, 3072:3168], LANES), _pad_cols(mu[:, 3168:3264], LANES), mu[:, 3264:3520]],
        axis=-1)
    zeros = jnp.zeros_like(rwkv_w0)
    pp = jnp.stack([mu[:, 0:1024], mu[:, 1024:2048], mu[:, 2048:3072], rwkv_w0, rwkv_a0, rwkv_k_k,
                    rwkv_k_a, rwkv_r_k.reshape(depth, RWKV_WIDTH), rwkv_ln_w, rwkv_ln_b]
                   + [zeros] * 6, axis=1)

    outs = []
    for bi in range(b):
        xb = x[bi]
        for l in range(depth):
            lambda_init = 0.8 - 0.6 * math.exp(-0.3 * l)
            proj = norm_matmul(xb, attn_norm_g[l], w_in_b[l])
            ya = diff_attention(proj, diff_lambda[l], diff_subln_g[l], diff_slopes,
                                lambda_init=lambda_init)
            yb = sliding_window_attention(proj, swa_sinks[l], swa_slopes)
            yc = rwkv7_mix(proj, pp[l], mu_lora[l:l + 1], w2_b[l], a2_b[l], g2_b[l])
            xb = matmul_residual([ya, yb, yc], w_out_b[l], xb, name="out_proj")
            hid = ffn_up(xb, ffn_norm_g[l], w_gu_b[l])
            xb = matmul_residual([hid], w_down_b[l], xb, name="ffn_down")
        outs.append(rmsnorm(xb, final_norm_g))
    return jnp.stack(outs, axis=0)
```

```python
import functools
import math

import numpy as np
import jax
import jax.numpy as jnp
from jax import lax
from jax.experimental import pallas as pl
from jax.experimental.pallas import tpu as pltpu

F32 = jnp.float32
BF16 = jnp.bfloat16

D_MODEL = 2048
LANES = 128
DIFF_HEADS = 4
DIFF_QK_DIM = 64
SWA_Q_HEADS = 8
SWA_HEAD_DIM = 64
SWA_WINDOW = 128
RWKV_HEADS = 16
RWKV_HEAD_DIM = 64
RWKV_WIDTH = RWKV_HEADS * RWKV_HEAD_DIM
RWKV_PAIRS = RWKV_WIDTH // LANES
DECAY_LORA = 96
ICLR_LORA = 96
GATE_LORA = 256
FFN_HIDDEN = 5632
ALIBI_HEADS = 12
NORM_EPS = 1e-5
RWKV_GN_EPS = 64e-5
NEG_INF = -1e30
CHUNK = 64
RWKV_GROUP = 4
PREP_UNROLL = 2

IN_COLS = 5824
PROJ_PAD = 256
PROJ_COLS = 6144
COL_QA, COL_KA, COL_VA = PROJ_PAD, PROJ_PAD + 512, PROJ_PAD + 1024
COL_QB, COL_KB, COL_VB = PROJ_PAD + 1536, PROJ_PAD + 2048, PROJ_PAD + 2176
COL_R, COL_K, COL_V = PROJ_PAD + 2304, PROJ_PAD + 3328, PROJ_PAD + 4352
COL_LORA = PROJ_PAD + 5376
LORA_COLS = 512

VMEM_LIMIT = 56 * 1024 * 1024


def _cparams(sem):
    return pltpu.CompilerParams(dimension_semantics=sem, vmem_limit_bytes=VMEM_LIMIT)


def _dot(a, b):
    return jnp.dot(a, b, preferred_element_type=F32)


def _dot_nt(a, b):
    return lax.dot_general(a, b, (((1,), (1,)), ((), ())), preferred_element_type=F32)


def _norm_to_scratch(x_ref, g_ref, h_ref):
    x = x_ref[...]
    ms = jnp.mean(x * x, axis=-1, keepdims=True)
    h_ref[...] = (x * lax.rsqrt(ms + NORM_EPS) * g_ref[...]).astype(BF16)


def _norm_mm_kernel(x_ref, g_ref, w_ref, o_ref, h_ref):
    @pl.when(pl.program_id(1) == 0)
    def _():
        _norm_to_scratch(x_ref, g_ref, h_ref)

    o_ref[...] = _dot(h_ref[...], w_ref[...])


def norm_matmul(x, g, w, *, tm=1024, tn=1024):
    t, d = x.shape
    n = w.shape[1]
    return pl.pallas_call(
        _norm_mm_kernel,
        out_shape=jax.ShapeDtypeStruct((t, n), F32),
        grid=(t // tm, n // tn),
        in_specs=[
            pl.BlockSpec((tm, d), lambda i, j: (i, 0)),
            pl.BlockSpec((1, d), lambda i, j: (0, 0)),
            pl.BlockSpec((d, tn), lambda i, j: (0, j)),
        ],
        out_specs=pl.BlockSpec((tm, tn), lambda i, j: (i, j)),
        scratch_shapes=[pltpu.VMEM((tm, d), BF16)],
        compiler_params=_cparams(("parallel", "arbitrary")),
        name="norm_in_proj",
    )(x, g.reshape(1, d), w)


def _ffn_up_kernel(x_ref, g_ref, wg_ref, wu_ref, o_ref, h_ref):
    @pl.when(pl.program_id(1) == 0)
    def _():
        _norm_to_scratch(x_ref, g_ref, h_ref)

    h = h_ref[...]
    gate = _dot(h, wg_ref[...])
    up = _dot(h, wu_ref[...])
    o_ref[...] = (gate / (1.0 + jnp.exp(-gate)) * up).astype(BF16)


def ffn_up(x, g, w_gate_up, *, tm=1024, tn=512):
    t, d = x.shape
    hid = w_gate_up.shape[1] // 2
    nj = hid // tn
    return pl.pallas_call(
        _ffn_up_kernel,
        out_shape=jax.ShapeDtypeStruct((t, hid), BF16),
        grid=(t // tm, nj),
        in_specs=[
            pl.BlockSpec((tm, d), lambda i, j: (i, 0)),
            pl.BlockSpec((1, d), lambda i, j: (0, 0)),
            pl.BlockSpec((d, tn), lambda i, j: (0, j)),
            pl.BlockSpec((d, tn), lambda i, j: (0, j + nj)),
        ],
        out_specs=pl.BlockSpec((tm, tn), lambda i, j: (i, j)),
        scratch_shapes=[pltpu.VMEM((tm, d), BF16)],
        compiler_params=_cparams(("parallel", "arbitrary")),
        name="ffn_up",
    )(x, g.reshape(1, d), w_gate_up, w_gate_up)


def _mm_res_kernel(*refs, splits):
    n = len(splits)
    a_refs, (w_ref, x_ref, o_ref) = refs[:n], refs[n:]
    acc = x_ref[...]
    off = 0
    for a_ref, k in zip(a_refs, splits):
        acc = acc + _dot(a_ref[...], w_ref[off:off + k, :])
        off += k
    o_ref[...] = acc


def matmul_residual(a_list, w, x, *, tm=1024, tn=512, name):
    t, n = x.shape
    splits = tuple(a.shape[1] for a in a_list)
    k = sum(splits)
    in_specs = [pl.BlockSpec((tm, s), lambda i, j: (i, 0)) for s in splits]
    in_specs += [
        pl.BlockSpec((k, tn), lambda i, j: (0, j)),
        pl.BlockSpec((tm, tn), lambda i, j: (i, j)),
    ]
    return pl.pallas_call(
        functools.partial(_mm_res_kernel, splits=splits),
        out_shape=jax.ShapeDtypeStruct((t, n), F32),
        grid=(t // tm, n // tn),
        in_specs=in_specs,
        out_specs=pl.BlockSpec((tm, tn), lambda i, j: (i, j)),
        compiler_params=_cparams(("parallel", "arbitrary")),
        name=name,
    )(*a_list, w, x)


def _rmsnorm_kernel(x_ref, g_ref, o_ref):
    x = x_ref[...]
    ms = jnp.mean(x * x, axis=-1, keepdims=True)
    o_ref[...] = x * lax.rsqrt(ms + NORM_EPS) * g_ref[...]


def rmsnorm(x, g, *, tm=512):
    t, d = x.shape
    return pl.pallas_call(
        _rmsnorm_kernel,
        out_shape=jax.ShapeDtypeStruct((t, d), F32),
        grid=(t // tm,),
        in_specs=[pl.BlockSpec((tm, d), lambda i: (i, 0)), pl.BlockSpec((1, d), lambda i: (0, 0))],
        out_specs=pl.BlockSpec((tm, d), lambda i: (i, 0)),
        compiler_params=_cparams(("parallel",)),
        name="final_rmsnorm",
    )(x, g.reshape(1, d))


def _diff_attn_kernel(slope_ref, q_ref, k_ref, v_ref, lam_ref, g_ref, o_ref,
                      kb_ref, vt_ref, qs_ref, brel_ref, m_ref, l_ref, acc_ref, *, tq, tsub, nk,
                      lambda_init):
    h = pl.program_id(0)
    i = pl.program_id(1)
    tk = tq
    slope = slope_ref[h]

    @pl.when(i == 0)
    def _():
        def stage(j, carry):
            rows = pl.ds(pl.multiple_of(j * tk, tk), tk)
            kb_ref[rows, :] = k_ref[rows, :].astype(BF16)
            vt_ref[j] = v_ref[rows, :].T.astype(BF16)
            return carry

        lax.fori_loop(0, nk, stage, 0)
        key = lax.broadcasted_iota(jnp.int32, (tk, tq), 0)
        qry = lax.broadcasted_iota(jnp.int32, (tk, tq), 1)
        brel_ref[...] = (key - qry).astype(F32) * slope

    q = q_ref[...] * (DIFF_QK_DIM ** -0.5)
    lane = lax.broadcasted_iota(jnp.int32, q.shape, 1)
    qs_ref[0] = jnp.where(lane < DIFF_QK_DIM, q, 0.0).astype(BF16)
    qs_ref[1] = jnp.where(lane < DIFF_QK_DIM, 0.0, q).astype(BF16)
    m_ref[...] = jnp.full(m_ref.shape, NEG_INF, F32)
    l_ref[...] = jnp.zeros(l_ref.shape, F32)
    acc_ref[...] = jnp.zeros(acc_ref.shape, F32)

    def block(j, masked):
        ks = kb_ref[pl.ds(pl.multiple_of(j * tk, tk), tk), :]
        vt = vt_ref[j]
        c0 = -slope * ((i - j) * tq).astype(F32)
        def one_part(c, qpart):
            qsl = slice(qpart * tsub, (qpart + 1) * tsub)
            s = _dot_nt(ks, qs_ref[c, qsl, :]) + brel_ref[:, qsl]
            if masked:
                key = lax.broadcasted_iota(jnp.int32, (tk, tsub), 0)
                qry = lax.broadcasted_iota(jnp.int32, (tk, tsub), 1) + qpart * tsub
                s = jnp.where(qry >= key, s, NEG_INF)
            yield
            m_prev = m_ref[c, :, qsl]
            m_new = jnp.maximum(m_prev, jnp.max(s, axis=0, keepdims=True) + c0)
            yield
            p = jnp.exp(s - (m_new - c0))
            alpha = jnp.exp(m_prev - m_new)
            l_new = alpha * l_ref[c, :, qsl] + jnp.sum(p, axis=0, keepdims=True)
            yield
            return l_new, alpha * acc_ref[c, :, qsl] + _dot(vt, p.astype(BF16)), m_new

        parts = [(c, qpart) for qpart in range(tq // tsub) for c in range(2)]
        results = _interleave([one_part(c, qpart) for c, qpart in parts])
        for (c, qpart), (l_new, acc_new, m_new) in zip(parts, results):
            qsl = slice(qpart * tsub, (qpart + 1) * tsub)
            l_ref[c, :, qsl] = l_new
            acc_ref[c, :, qsl] = acc_new
            m_ref[c, :, qsl] = m_new

    def body(j, carry):
        block(j, False)
        return carry

    lax.fori_loop(0, i, body, 0)
    block(i, True)

    lamv = lam_ref[...]
    s01 = jnp.sum(lamv[0:1] * lamv[1:2], axis=-1, keepdims=True)
    s23 = jnp.sum(lamv[2:3] * lamv[3:4], axis=-1, keepdims=True)
    lam = jnp.exp(s01) - jnp.exp(s23) + lambda_init
    ot = acc_ref[0] / l_ref[0] - lam * (acc_ref[1] / l_ref[1])
    o = ot.T
    o = o * lax.rsqrt(jnp.mean(o * o, axis=-1, keepdims=True) + NORM_EPS)
    o_ref[...] = (o * g_ref[...] * (1.0 - lambda_init)).astype(BF16)


def diff_attention(proj, lamv, subln_g, slopes, *, lambda_init, tq=512, tsub=512):
    t = proj.shape[0]
    nk = t // tq
    cb = lambda col: col // LANES
    return pl.pallas_call(
        functools.partial(_diff_attn_kernel, tq=tq, tsub=tsub, nk=nk, lambda_init=lambda_init),
        out_shape=jax.ShapeDtypeStruct((t, DIFF_HEADS * LANES), BF16),
        grid=(DIFF_HEADS, t // tq),
        in_specs=[
            pl.BlockSpec(memory_space=pltpu.SMEM),
            pl.BlockSpec((tq, LANES), lambda h, i: (i, cb(COL_QA) + h)),
            pl.BlockSpec((t, LANES), lambda h, i: (0, cb(COL_KA) + h)),
            pl.BlockSpec((t, LANES), lambda h, i: (0, cb(COL_VA) + h)),
            pl.BlockSpec((4, DIFF_QK_DIM), lambda h, i: (0, 0)),
            pl.BlockSpec((1, LANES), lambda h, i: (0, 0)),
        ],
        out_specs=pl.BlockSpec((tq, LANES), lambda h, i: (i, h)),
        scratch_shapes=[
            pltpu.VMEM((t, LANES), BF16),
            pltpu.VMEM((nk, LANES, tq), BF16),
            pltpu.VMEM((2, tq, LANES), BF16),
            pltpu.VMEM((tq, tq), F32),
            pltpu.VMEM((2, 1, tq), F32),
            pltpu.VMEM((2, 1, tq), F32),
            pltpu.VMEM((2, LANES, tq), F32),
        ],
        compiler_params=_cparams(("arbitrary", "arbitrary")),
        name="diff_attention",
    )(slopes, proj, proj, proj, lamv, subln_g.reshape(1, LANES))


def _swa_kernel(sink_ref, qa_ref, qb_ref, k_ref, v_ref, kh_ref, vh_ref, o_ref, *, tq, slopes):
    i = pl.program_id(0)
    w = SWA_WINDOW
    low = lax.broadcasted_iota(jnp.int32, (w, LANES), 1) < SWA_HEAD_DIM
    low2 = lax.broadcasted_iota(jnp.int32, (2 * w, LANES), 1) < SWA_HEAD_DIM
    row = lax.broadcasted_iota(jnp.int32, (w, 2 * w), 0)
    col = lax.broadcasted_iota(jnp.int32, (w, 2 * w), 1)
    dist = row + w - col
    valid = jnp.logical_and(dist >= 0, dist < w)
    valid0 = jnp.logical_and(valid, jnp.logical_or(col >= w, i > 0))
    distf = dist.astype(F32)
    q_refs = (qa_ref, qb_ref)
    for r in range(tq // w):
        rs = slice(r * w, (r + 1) * w)
        ok = valid0 if r == 0 else valid
        if r == 0:
            kband = jnp.concatenate([kh_ref[...], k_ref[0:w, :]], axis=0)
            vband = jnp.concatenate([vh_ref[...], v_ref[0:w, :]], axis=0)
        else:
            kband = k_ref[(r - 1) * w:(r + 1) * w, :]
            vband = v_ref[(r - 1) * w:(r + 1) * w, :]
        kroll = pltpu.roll(kband, SWA_HEAD_DIM, axis=1)
        vroll = pltpu.roll(vband, SWA_HEAD_DIM, axis=1)
        for kv in range(2):
            kd = (jnp.where(low2, kband, kroll) if kv == 0 else jnp.where(low2, kroll, kband))
            vd = (jnp.where(low2, vband, vroll) if kv == 0 else jnp.where(low2, vroll, vband))
            kd = kd.astype(BF16)
            vd = vd.astype(BF16)
            for j in range(2):
                jj = kv * 2 + j
                q = q_refs[kv][rs, j * LANES:(j + 1) * LANES] * (SWA_HEAD_DIM ** -0.5)
                outs = []
                for half in range(2):
                    hq = 2 * jj + half
                    qh = jnp.where(low, q, 0.0) if half == 0 else jnp.where(low, 0.0, q)
                    s = _dot_nt(qh.astype(BF16), kd)
                    s = jnp.where(ok, s - slopes[hq] * distf, NEG_INF)
                    sink = sink_ref[hq]
                    m = jnp.maximum(jnp.max(s, axis=-1, keepdims=True), sink)
                    e = jnp.exp(s - m)
                    den = jnp.sum(e, axis=-1, keepdims=True) + jnp.exp(sink - m)
                    outs.append(_dot((e / den).astype(BF16), vd))
                o_ref[rs, jj * LANES:(jj + 1) * LANES] = jnp.where(low, outs[0], outs[1]).astype(BF16)


def sliding_window_attention(proj, sinks, slopes, *, tq=512):
    t = proj.shape[0]
    w = SWA_WINDOW
    qw = SWA_Q_HEADS * SWA_HEAD_DIM
    rpb = tq // w
    qblk = COL_QB // (2 * LANES)
    kblk, vblk = COL_KB // LANES, COL_VB // LANES
    return pl.pallas_call(
        functools.partial(_swa_kernel, tq=tq, slopes=tuple(float(s) for s in slopes)),
        out_shape=jax.ShapeDtypeStruct((t, qw), BF16),
        grid=(t // tq,),
        in_specs=[
            pl.BlockSpec(memory_space=pltpu.SMEM),
            pl.BlockSpec((tq, 2 * LANES), lambda i: (i, qblk)),
            pl.BlockSpec((tq, 2 * LANES), lambda i: (i, qblk + 1)),
            pl.BlockSpec((tq, LANES), lambda i: (i, kblk)),
            pl.BlockSpec((tq, LANES), lambda i: (i, vblk)),
            pl.BlockSpec((w, LANES), lambda i: (jnp.maximum(i * rpb - 1, 0), kblk)),
            pl.BlockSpec((w, LANES), lambda i: (jnp.maximum(i * rpb - 1, 0), vblk)),
        ],
        out_specs=pl.BlockSpec((tq, qw), lambda i: (i, 0)),
        compiler_params=_cparams(("parallel",)),
        name="swa_attention",
    )(sinks, proj, proj, proj, proj, proj, proj)


def _interleave(gens):
    gens = list(gens)
    results = [None] * len(gens)
    live = list(range(len(gens)))
    while live:
        still = []
        for idx in live:
            try:
                next(gens[idx])
                still.append(idx)
            except StopIteration as done:
                results[idx] = done.value
        live = still
    return results


def _seg_sum(x, low):
    s0 = jnp.sum(jnp.where(low, x, 0.0), axis=-1, keepdims=True)
    s1 = jnp.sum(jnp.where(low, 0.0, x), axis=-1, keepdims=True)
    return jnp.where(low, s0, s1)


def _stack_heads(z, low):
    zero = jnp.zeros_like(z)
    return jnp.concatenate([jnp.where(low, z, zero), jnp.where(low, zero, z)], axis=0)


def _rwkv_kernel(r_ref, k_ref, v_ref, lo_ref, rh_ref, kh_ref, vh_ref, loh_ref, pp_ref, mul_ref,
                 w2_ref, a2_ref, g2_ref, y_ref,
                 h_ref, at_ref, rt_ref, bt_ref, kt_ref, bh_ref, khat_ref, vs_ref, et_ref,
                 wr_ref, u0_ref, y0_ref, arb_ref, bht_ref, kv_ref, dx_ref, ys_ref, *, tt, grp):
    i = pl.program_id(1)
    nch = tt // CHUNK
    c2 = 2 * CHUNK
    gw = grp * LANES

    @pl.when(i == 0)
    def _():
        h_ref[...] = jnp.zeros(h_ref.shape, F32)

    def shift(x_ref, halo_ref, mu):
        x = x_ref[...]
        rowi = lax.broadcasted_iota(jnp.int32, x.shape, 0)
        first = jnp.where(i > 0, halo_ref[7:8, :], 0.0)
        prev = jnp.where(rowi == 0, first, pltpu.roll(x, 1, axis=0))
        return x + (prev - x) * mu

    pp = pp_ref[...]
    mu_r, mu_k, mu_v = pp[0:1], pp[1:2], pp[2:3]
    w0, a0, k_k, k_a, r_k, ln_w, ln_b = (pp[3:4], pp[4:5], pp[5:6], pp[6:7], pp[7:8], pp[8:9],
                                         pp[9:10])
    r = shift(r_ref, rh_ref, mu_r)
    k = shift(k_ref, kh_ref, mu_k)
    v = shift(v_ref, vh_ref, mu_v)
    lo = shift(lo_ref, loh_ref, mul_ref[...])
    z = -(w0 + _dot(jnp.tanh(lo[:, 0:LANES]).astype(BF16), w2_ref[...]))
    softplus = jnp.maximum(z, 0.0) + jnp.log(1.0 + jnp.exp(-jnp.abs(z)))
    lw = -jnp.exp(-softplus - 0.5)
    a = 1.0 / (1.0 + jnp.exp(-(a0 + _dot(lo[:, 0:2 * LANES].astype(BF16), a2_ref[...]))))
    g = _dot((1.0 / (1.0 + jnp.exp(-lo[:, LANES:4 * LANES]))).astype(BF16), g2_ref[...])

    low = lax.broadcasted_iota(jnp.int32, (tt, LANES), 1) < RWKV_HEAD_DIM
    rowc = lax.broadcasted_iota(jnp.int32, (tt, gw), 0) & (CHUNK - 1)

    def seg_sum(x):
        return jnp.concatenate(
            [_seg_sum(x[:, p * LANES:(p + 1) * LANES], low) for p in range(grp)], axis=1)

    kx = k * k_k
    kk = kx * lax.rsqrt(jnp.maximum(seg_sum(kx * kx), 1e-24))
    k2 = k * (1.0 + (a - 1.0) * k_a)
    bvec = kk * a

    cum = lw
    for s in (1, 2, 4, 8, 16, 32):
        cum = cum + jnp.where(rowc >= s, pltpu.roll(cum, s, axis=0), 0.0)
    tot = jnp.broadcast_to(cum.reshape(nch, CHUNK, gw)[:, CHUNK - 1:CHUNK, :],
                           (nch, CHUNK, gw)).reshape(tt, gw)

    einv = jnp.exp(-cum)
    erem = jnp.exp(tot - cum)
    at_ref[...] = (-kk * jnp.exp(cum - lw)).astype(BF16)
    rt_ref[...] = (r * jnp.exp(cum)).astype(BF16)
    bt_ref[...] = (bvec * einv).astype(BF16)
    kt_ref[...] = (k2 * einv).astype(BF16)
    bh_ref[...] = bvec * erem
    khat_ref[...] = k2 * erem
    vs_ref[...] = v.astype(BF16)
    et_ref[...] = jnp.exp(tot)

    low_c = lax.broadcasted_iota(jnp.int32, (CHUNK, LANES), 1) < RWKV_HEAD_DIM
    rr = lax.broadcasted_iota(jnp.int32, (CHUNK, LANES), 0)
    cc = lax.broadcasted_iota(jnp.int32, (CHUNK, LANES), 1) & (RWKV_HEAD_DIM - 1)
    lt = rr > cc
    le = rr >= cc
    eye = jnp.where(rr == cc, 1.0, 0.0)
    bd = (lax.broadcasted_iota(jnp.int32, (c2, LANES), 0) < CHUNK) == (
        lax.broadcasted_iota(jnp.int32, (c2, LANES), 1) < RWKV_HEAD_DIM)

    def prepare_chain(p, c):
        sl = pl.ds(pl.multiple_of(c * CHUNK, CHUNK), CHUNK)
        ln = slice(p * LANES, (p + 1) * LANES)
        at, rt, vs = at_ref[sl, ln], rt_ref[sl, ln], vs_ref[sl, ln]
        sb = _stack_heads(bt_ref[sl, ln], low_c)
        sk = _stack_heads(kt_ref[sl, ln], low_c)
        sv = _stack_heads(vs, low_c)
        a4 = _dot_nt(jnp.concatenate([at, rt], axis=0), jnp.concatenate([sb, sk], axis=0))
        kv = jnp.where(bd, _dot(khat_ref[sl, ln].T.astype(BF16), vs), 0.0)
        bht = bh_ref[sl, ln].T.astype(BF16)
        dx = jnp.broadcast_to(et_ref[sl, ln][0:1, :], (c2, LANES)).T
        yield
        n = jnp.where(lt, a4[:CHUNK, :LANES], 0.0)
        aak = jnp.where(lt, a4[:CHUNK, LANES:], 0.0)
        arb = jnp.where(le, a4[CHUNK:, :LANES], 0.0)
        ark = jnp.where(le, a4[CHUNK:, LANES:], 0.0)
        av = _dot(jnp.concatenate([aak, ark], axis=0).astype(BF16), sv)
        tinv = eye + n
        pw = n
        for _ in range(int(math.log2(CHUNK)) - 1):
            pb = pw.astype(BF16)
            pw = _dot(pb, _stack_heads(pb, low_c))
            yield
            tinv = tinv + _dot(pw.astype(BF16), _stack_heads(tinv.astype(BF16), low_c))
            yield
        rhs = jnp.concatenate([_stack_heads(at, low_c),
                               _stack_heads(av[:CHUNK].astype(BF16), low_c)], axis=1)
        wu = _dot(tinv.astype(BF16), rhs)
        yield
        return (jnp.concatenate([wu[:, :LANES].astype(BF16), rt], axis=0), wu[:, LANES:],
                av[CHUNK:], arb.astype(BF16), bht, kv, dx)

    def prepare(cc, carry):
        chains = [(p, cc * PREP_UNROLL + u) for u in range(PREP_UNROLL) for p in range(grp)]
        results = _interleave([prepare_chain(p, c) for p, c in chains])
        for (p, c), vals in zip(chains, results):
            for ref, val in zip((wr_ref, u0_ref, y0_ref, arb_ref, bht_ref, kv_ref, dx_ref), vals):
                ref[p, c] = val
        return carry

    lax.fori_loop(0, nch // PREP_UNROLL, prepare, 0)

    def advance_chain(p, c):
        h = h_ref[p]
        wh = _dot(wr_ref[p, c], h.astype(BF16))
        yield
        u = (wh[:CHUNK] + u0_ref[p, c]).astype(BF16)
        y = wh[CHUNK:] + _dot(arb_ref[p, c], _stack_heads(u, low_c)) + y0_ref[p, c]
        h = dx_ref[p, c] * h + jnp.where(bd, _dot(bht_ref[p, c], u), 0.0) + kv_ref[p, c]
        yield
        return y, h

    def advance(c, carry):
        sl = pl.ds(pl.multiple_of(c * CHUNK, CHUNK), CHUNK)
        results = _interleave([advance_chain(p, c) for p in range(grp)])
        for p, (y, h) in enumerate(results):
            ys_ref[sl, p * LANES:(p + 1) * LANES] = y
            h_ref[p] = h
        return carry

    lax.fori_loop(0, nch, advance, 0)

    y = ys_ref[...]
    mean = seg_sum(y) * (1.0 / RWKV_HEAD_DIM)
    yc = y - mean
    var = seg_sum(yc * yc) * (1.0 / RWKV_HEAD_DIM)
    yn = yc * lax.rsqrt(var + RWKV_GN_EPS) * ln_w + ln_b
    bonus = seg_sum(r * k2 * r_k) * v
    y_ref[...] = ((yn + bonus) * g).astype(BF16)


def rwkv7_mix(proj, pp, mu_lora, w2p, a2p, g2p, *, tt=512, grp=RWKV_GROUP):
    t = proj.shape[0]
    nch = tt // CHUNK
    c2 = 2 * CHUNK
    gw = grp * LANES
    hb = tt // 8
    cb = lambda col: col // gw
    lb = COL_LORA // LORA_COLS
    halo = lambda col: (lambda p, i: (jnp.maximum(i * hb - 1, 0), cb(col) + p))
    in_specs = [
        pl.BlockSpec((tt, gw), lambda p, i: (i, cb(COL_R) + p)),
        pl.BlockSpec((tt, gw), lambda p, i: (i, cb(COL_K) + p)),
        pl.BlockSpec((tt, gw), lambda p, i: (i, cb(COL_V) + p)),
        pl.BlockSpec((tt, LORA_COLS), lambda p, i: (i, lb)),
        pl.BlockSpec((8, gw), halo(COL_R)),
        pl.BlockSpec((8, gw), halo(COL_K)),
        pl.BlockSpec((8, gw), halo(COL_V)),
        pl.BlockSpec((8, LORA_COLS), lambda p, i: (jnp.maximum(i * hb - 1, 0), lb)),
        pl.BlockSpec((16, gw), lambda p, i: (0, p)),
        pl.BlockSpec((1, LORA_COLS), lambda p, i: (0, 0)),
        pl.BlockSpec((LANES, gw), lambda p, i: (0, p)),
        pl.BlockSpec((2 * LANES, gw), lambda p, i: (0, p)),
        pl.BlockSpec((3 * LANES, gw), lambda p, i: (0, p)),
    ]
    scratch = [
        pltpu.VMEM((grp, c2, LANES), F32),
        pltpu.VMEM((tt, gw), BF16),
        pltpu.VMEM((tt, gw), BF16),
        pltpu.VMEM((tt, gw), BF16),
        pltpu.VMEM((tt, gw), BF16),
        pltpu.VMEM((tt, gw), F32),
        pltpu.VMEM((tt, gw), F32),
        pltpu.VMEM((tt, gw), BF16),
        pltpu.VMEM((tt, gw), F32),
        pltpu.VMEM((grp, nch, c2, LANES), BF16),
        pltpu.VMEM((grp, nch, CHUNK, LANES), F32),
        pltpu.VMEM((grp, nch, CHUNK, LANES), F32),
        pltpu.VMEM((grp, nch, CHUNK, LANES), BF16),
        pltpu.VMEM((grp, nch, LANES, CHUNK), BF16),
        pltpu.VMEM((grp, nch, c2, LANES), F32),
        pltpu.VMEM((grp, nch, c2, LANES), F32),
        pltpu.VMEM((tt, gw), F32),
    ]
    return pl.pallas_call(
        functools.partial(_rwkv_kernel, tt=tt, grp=grp),
        out_shape=jax.ShapeDtypeStruct((t, RWKV_WIDTH), BF16),
        grid=(RWKV_PAIRS // grp, t // tt),
        in_specs=in_specs,
        out_specs=pl.BlockSpec((tt, gw), lambda p, i: (i, p)),
        scratch_shapes=scratch,
        compiler_params=_cparams(("arbitrary", "arbitrary")),
        name="rwkv7_mix",
    )(proj, proj, proj, proj, proj, proj, proj, proj, pp, mu_lora, w2p, a2p, g2p)


def _alibi_slopes():
    idx = np.arange(1, ALIBI_HEADS + 1, dtype=np.float32)
    m = np.exp2(-8.0 * idx / ALIBI_HEADS).astype(np.float32)
    diff_idx = np.arange(2, ALIBI_HEADS, 3)
    swa_idx = np.setdiff1d(np.arange(ALIBI_HEADS), diff_idx)
    return m[diff_idx], m[swa_idx]


def _pad_rows(w, before, total):
    return jnp.pad(w, ((0, 0), (before, total - before - w.shape[1]), (0, 0)))


def kernel(x, attn_norm_g, w_in, diff_lambda, diff_subln_g, swa_sinks, rwkv_mu, rwkv_w0, rwkv_w2,
           rwkv_a0, rwkv_a2, rwkv_g2, rwkv_k_k, rwkv_k_a, rwkv_r_k, rwkv_ln_w, rwkv_ln_b, w_out,
           ffn_norm_g, w_gate_up, w_down, final_norm_g):
    b, t, d = x.shape
    depth = w_in.shape[0]
    diff_slopes, swa_slopes = _alibi_slopes()
    diff_slopes = jnp.asarray(diff_slopes)

    w_in_b = jnp.pad(w_in.astype(BF16), ((0, 0), (0, 0), (PROJ_PAD, PROJ_COLS - PROJ_PAD - IN_COLS)))
    w_out_b = w_out.astype(BF16)
    w_gu_b = w_gate_up.astype(BF16)
    w_down_b = w_down.astype(BF16)
    w2_b = _pad_rows(rwkv_w2, 0, LANES).astype(BF16)
    a2_b = _pad_rows(rwkv_a2, DECAY_LORA, 2 * LANES).astype(BF16)
    g2_b = _pad_rows(rwkv_g2, DECAY_LORA + ICLR_LORA - LANES, 3 * LANES).astype(BF16)
    mu = rwkv_mu
    mu_lora = jnp.pad(mu[:, 3 * RWKV_WIDTH:], ((0, 0), (0, LORA_COLS - (mu.shape[1] - 3 * RWKV_WIDTH))))
    zeros = jnp.zeros_like(rwkv_w0)
    pp = jnp.stack([mu[:, 0:1024], mu[:, 1024:2048], mu[:, 2048:3072], rwkv_w0, rwkv_a0, rwkv_k_k,
                    rwkv_k_a, rwkv_r_k.reshape(depth, RWKV_WIDTH), rwkv_ln_w, rwkv_ln_b]
                   + [zeros] * 6, axis=1)

    outs = []
    for bi in range(b):
        xb = x[bi]
        for l in range(depth):
            lambda_init = 0.8 - 0.6 * math.exp(-0.3 * l)
            proj = norm_matmul(xb, attn_norm_g[l], w_in_b[l])
            ya = diff_attention(proj, diff_lambda[l], diff_subln_g[l], diff_slopes,
                                lambda_init=lambda_init)
            yb = sliding_window_attention(proj, swa_sinks[l], swa_slopes)
            yc = rwkv7_mix(proj, pp[l], mu_lora[l:l + 1], w2_b[l], a2_b[l], g2_b[l])
            xb = matmul_residual([ya, yb, yc], w_out_b[l], xb, name="out_proj")
            hid = ffn_up(xb, ffn_norm_g[l], w_gu_b[l])
            xb = matmul_residual([hid], w_down_b[l], xb, name="ffn_down")
        outs.append(rmsnorm(xb, final_norm_g))
    return jnp.stack(outs, axis=0)
```

```python
import functools
import math

import numpy as np
import jax
import jax.numpy as jnp
from jax import lax
from jax.experimental import pallas as pl
from jax.experimental.pallas import tpu as pltpu

F32 = jnp.float32
BF16 = jnp.bfloat16

D_MODEL = 2048
LANES = 128
SUBLANES = 8
DIFF_HEADS = 4
DIFF_QK_DIM = 64
SWA_Q_HEADS = 8
SWA_HEAD_DIM = 64
SWA_WINDOW = 128
RWKV_HEADS = 16
RWKV_HEAD_DIM = 64
RWKV_WIDTH = RWKV_HEADS * RWKV_HEAD_DIM
RWKV_PAIRS = RWKV_WIDTH // LANES
DECAY_LORA = 96
ICLR_LORA = 96
GATE_LORA = 256
FFN_HIDDEN = 5632
ALIBI_HEADS = 12
NORM_EPS = 1e-5
RWKV_GN_EPS = 64e-5
NEG_INF = -1e30
CHUNK = 64
RWKV_GROUP = 4
PREP_UNROLL = 8

IN_COLS = 5824
PROJ_PAD = 256
PROJ_COLS = 6144
COL_QA, COL_KA, COL_VA = PROJ_PAD, PROJ_PAD + 512, PROJ_PAD + 1024
COL_QB, COL_KB, COL_VB = PROJ_PAD + 1536, PROJ_PAD + 2048, PROJ_PAD + 2176
COL_R, COL_K, COL_V = PROJ_PAD + 2304, PROJ_PAD + 3328, PROJ_PAD + 4352
COL_LORA = PROJ_PAD + 5376
LORA_COLS = 512

VMEM_LIMIT = 56 * 1024 * 1024


def _cparams(sem):
    return pltpu.CompilerParams(dimension_semantics=sem, vmem_limit_bytes=VMEM_LIMIT)


def _dot(a, b):
    return jnp.dot(a, b, preferred_element_type=F32)


def _dot_nt(a, b):
    return lax.dot_general(a, b, (((1,), (1,)), ((), ())), preferred_element_type=F32)


def _norm_to_scratch(x_ref, g_ref, h_ref):
    x = x_ref[...]
    ms = jnp.mean(x * x, axis=-1, keepdims=True)
    h_ref[...] = (x * lax.rsqrt(ms + NORM_EPS) * g_ref[...]).astype(BF16)


def _norm_mm_kernel(x_ref, g_ref, w_ref, mu_ref, o_ref, h_ref, last_ref):
    i = pl.program_id(0)
    j = pl.program_id(1)

    @pl.when(j == 0)
    def _():
        _norm_to_scratch(x_ref, g_ref, h_ref)

    acc = _dot(h_ref[...], w_ref[...])
    rowi = lax.broadcasted_iota(jnp.int32, acc.shape, 0)
    first = jnp.where(i > 0, last_ref[j, SUBLANES - 1:SUBLANES, :], 0.0)
    prev = jnp.where(rowi == 0, first, pltpu.roll(acc, 1, axis=0))
    last_ref[j] = acc[acc.shape[0] - SUBLANES:, :]
    o_ref[...] = acc + (prev - acc) * mu_ref[...]


def norm_matmul(x, g, w, mu, layer, *, tm=1024, tn=1024):
    t, d = x.shape
    n = w.shape[2]
    return pl.pallas_call(
        _norm_mm_kernel,
        out_shape=jax.ShapeDtypeStruct((t, n), F32),
        grid=(t // tm, n // tn),
        in_specs=[
            pl.BlockSpec((tm, d), lambda i, j: (i, 0)),
            pl.BlockSpec((None, 1, d), lambda i, j: (layer, 0, 0)),
            pl.BlockSpec((None, d, tn), lambda i, j: (layer, 0, j)),
            pl.BlockSpec((None, 1, tn), lambda i, j: (layer, 0, j)),
        ],
        out_specs=pl.BlockSpec((tm, tn), lambda i, j: (i, j)),
        scratch_shapes=[pltpu.VMEM((tm, d), BF16), pltpu.VMEM((n // tn, SUBLANES, tn), F32)],
        compiler_params=_cparams(("arbitrary", "arbitrary")),
        name="norm_in_proj",
    )(x, g, w, mu)


def _ffn_up_kernel(x_ref, g_ref, wg_ref, wu_ref, o_ref, h_ref):
    @pl.when(pl.program_id(1) == 0)
    def _():
        _norm_to_scratch(x_ref, g_ref, h_ref)

    h = h_ref[...]
    gate = _dot(h, wg_ref[...])
    up = _dot(h, wu_ref[...])
    o_ref[...] = (gate / (1.0 + jnp.exp(-gate)) * up).astype(BF16)


def ffn_up(x, g, w_gate_up, layer, *, tm=1024, tn=512):
    t, d = x.shape
    hid = w_gate_up.shape[2] // 2
    nj = hid // tn
    return pl.pallas_call(
        _ffn_up_kernel,
        out_shape=jax.ShapeDtypeStruct((t, hid), BF16),
        grid=(t // tm, nj),
        in_specs=[
            pl.BlockSpec((tm, d), lambda i, j: (i, 0)),
            pl.BlockSpec((None, 1, d), lambda i, j: (layer, 0, 0)),
            pl.BlockSpec((None, d, tn), lambda i, j: (layer, 0, j)),
            pl.BlockSpec((None, d, tn), lambda i, j: (layer, 0, j + nj)),
        ],
        out_specs=pl.BlockSpec((tm, tn), lambda i, j: (i, j)),
        scratch_shapes=[pltpu.VMEM((tm, d), BF16)],
        compiler_params=_cparams(("parallel", "arbitrary")),
        name="ffn_up",
    )(x, g, w_gate_up, w_gate_up)


def _mm_res_kernel(*refs, splits):
    n = len(splits)
    a_refs, (w_ref, x_ref, o_ref) = refs[:n], refs[n:]
    acc = x_ref[...]
    off = 0
    for a_ref, k in zip(a_refs, splits):
        acc = acc + _dot(a_ref[...], w_ref[off:off + k, :])
        off += k
    o_ref[...] = acc


def matmul_residual(a_list, w, layer, x, *, tm=1024, tn=512, name):
    t, n = x.shape
    splits = tuple(a.shape[1] for a in a_list)
    k = sum(splits)
    in_specs = [pl.BlockSpec((tm, s), lambda i, j: (i, 0)) for s in splits]
    in_specs += [
        pl.BlockSpec((None, k, tn), lambda i, j: (layer, 0, j)),
        pl.BlockSpec((tm, tn), lambda i, j: (i, j)),
    ]
    return pl.pallas_call(
        functools.partial(_mm_res_kernel, splits=splits),
        out_shape=jax.ShapeDtypeStruct((t, n), F32),
        grid=(t // tm, n // tn),
        in_specs=in_specs,
        out_specs=pl.BlockSpec((tm, tn), lambda i, j: (i, j)),
        compiler_params=_cparams(("parallel", "arbitrary")),
        name=name,
    )(*a_list, w, x)


def _rmsnorm_kernel(x_ref, g_ref, o_ref):
    x = x_ref[...]
    ms = jnp.mean(x * x, axis=-1, keepdims=True)
    o_ref[...] = x * lax.rsqrt(ms + NORM_EPS) * g_ref[...]


def rmsnorm(x, g, *, tm=512):
    t, d = x.shape
    return pl.pallas_call(
        _rmsnorm_kernel,
        out_shape=jax.ShapeDtypeStruct((t, d), F32),
        grid=(t // tm,),
        in_specs=[pl.BlockSpec((tm, d), lambda i: (i, 0)), pl.BlockSpec((1, d), lambda i: (0, 0))],
        out_specs=pl.BlockSpec((tm, d), lambda i: (i, 0)),
        compiler_params=_cparams(("parallel",)),
        name="final_rmsnorm",
    )(x, g.reshape(1, d))


LOG2E = 1.4426950408889634
POS_SPLIT = 128
ONES_ROWS = 16


def _pos_lanes(lane, base, cols):
    out = jnp.zeros(lane.shape, F32)
    for idx, col in enumerate(cols):
        out = jnp.where(lane == base + idx, col, out)
    return out


def _diff_attn_kernel(slope_ref, q_ref, k_ref, v_ref, lam_ref, g_ref, o_ref,
                      kb_ref, vt_ref, qs_ref, sa_ref, sb_ref, m_ref, acc_ref, *, tq, tk, nk,
                      lambda_init):
    h = pl.program_id(0)
    i = pl.program_id(1)
    kpq = tq // tk
    beta = jnp.full((1, LANES), slope_ref[h] * LOG2E, F32)
    b1 = beta.astype(BF16).astype(F32)
    b2 = (beta - b1).astype(BF16).astype(F32)
    b3 = (beta - b1 - b2).astype(BF16).astype(F32)
    own = [lambda lane, c=c: (lane < DIFF_QK_DIM) == (c == 0) for c in range(2)]
    pos_base = [DIFF_QK_DIM, 0]

    @pl.when(i == 0)
    def _():
        lane = lax.broadcasted_iota(jnp.int32, (tk, LANES), 1)
        krel = lax.broadcasted_iota(jnp.int32, (tk, LANES), 0)
        khi = ((krel // POS_SPLIT) * POS_SPLIT).astype(F32)
        klo = (krel % POS_SPLIT).astype(F32)
        kcols = []
        for b in (b1, b2, b3):
            kcols += [-b * POS_SPLIT, -b, khi, klo]
        posk = [_pos_lanes(lane, pos_base[c], kcols) for c in range(2)]
        ones = jnp.where(lax.broadcasted_iota(jnp.int32, (ONES_ROWS, tk), 0) == 0, 1.0, 0.0)

        def stage(j, carry):
            rows = pl.ds(pl.multiple_of(j * tk, tk), tk)
            k = k_ref[rows, :]
            for c in range(2):
                kb_ref[c, rows, :] = jnp.where(own[c](lane), k, posk[c]).astype(BF16)
            vt_ref[j] = jnp.concatenate([v_ref[rows, :].T, ones], axis=0).astype(BF16)
            return carry

        lax.fori_loop(0, nk, stage, 0)

    q = q_ref[...] * (DIFF_QK_DIM ** -0.5 * LOG2E)
    lane = lax.broadcasted_iota(jnp.int32, (tq, LANES), 1)
    qrel = lax.broadcasted_iota(jnp.int32, (tq, LANES), 0)
    qhi = (qrel // POS_SPLIT).astype(F32)
    qlo = (qrel % POS_SPLIT).astype(F32)
    qcols = []
    for b in (b1, b2, b3):
        qcols += [qhi, qlo, b, b]
    for c in range(2):
        qs_ref[c] = jnp.where(own[c](lane), q, _pos_lanes(lane, pos_base[c], qcols)).astype(BF16)
    m_ref[...] = jnp.full(m_ref.shape, NEG_INF, F32)
    acc_ref[...] = jnp.zeros(acc_ref.shape, F32)

    def scores(j, dst_ref):
        rows = pl.ds(pl.multiple_of(j * tk, tk), tk)

        def chain(c):
            s = _dot_nt(kb_ref[c, rows, :], qs_ref[c])
            yield
            return s

        return [chain(c) for c in range(2)], dst_ref

    def consume(src_ref, j, diag):
        vt = vt_ref[j]
        c0 = (slope_ref[h] * LOG2E) * (j * tk - i * tq).astype(F32)

        def chain(c):
            s = src_ref[c]
            if diag is not None:
                key = lax.broadcasted_iota(jnp.int32, (tk, tq), 0) + diag * tk
                qry = lax.broadcasted_iota(jnp.int32, (tk, tq), 1)
                s = jnp.where(qry >= key, s, NEG_INF)
            m_prev = m_ref[c]
            m_new = jnp.maximum(m_prev, jnp.max(s, axis=0, keepdims=True) + c0)
            yield
            p = jnp.exp2(s - (m_new - c0)).astype(BF16)
            alpha = jnp.exp2(m_prev - m_new)
            yield
            return alpha * acc_ref[c] + _dot(vt, p), m_new

        return [chain(c) for c in range(2)]

    def step(score_args, consume_args):
        chains = consume(*consume_args)
        dst = None
        if score_args is not None:
            score_chains, dst = scores(*score_args)
            chains = chains + score_chains
        results = _interleave(chains)
        for c in range(2):
            acc_new, m_new = results[c]
            acc_ref[c] = acc_new
            m_ref[c] = m_new
            if dst is not None:
                dst[c] = results[2 + c]

    for c, s in enumerate(_interleave(scores(0, sa_ref)[0])):
        sa_ref[c] = s

    def body(jj, carry):
        j = jj * kpq
        step((j + 1, sb_ref), (sa_ref, j, None))
        step((j + 2, sa_ref), (sb_ref, j + 1, None))
        return carry

    lax.fori_loop(0, i, body, 0)
    step((i * kpq + 1, sb_ref), (sa_ref, i * kpq, 0))
    step(None, (sb_ref, i * kpq + 1, 1))

    lamv = lam_ref[...]
    s01 = jnp.sum(lamv[0:1] * lamv[1:2], axis=-1, keepdims=True)
    s23 = jnp.sum(lamv[2:3] * lamv[3:4], axis=-1, keepdims=True)
    lam = jnp.exp(s01) - jnp.exp(s23) + lambda_init
    a0, a1 = acc_ref[0], acc_ref[1]
    ot = a0[:LANES] / a0[LANES:LANES + 1] - lam * (a1[:LANES] / a1[LANES:LANES + 1])
    o = ot.T
    o = o * lax.rsqrt(jnp.mean(o * o, axis=-1, keepdims=True) + NORM_EPS)
    o_ref[...] = (o * g_ref[...] * (1.0 - lambda_init)).astype(BF16)


def diff_attention(proj, lamv, subln_g, slopes, *, lambda_init, tq=1024, tk=512):
    t = proj.shape[0]
    nk = t // tk
    assert tq == 2 * tk and t % tq == 0
    cb = lambda col: col // LANES
    return pl.pallas_call(
        functools.partial(_diff_attn_kernel, tq=tq, tk=tk, nk=nk, lambda_init=lambda_init),
        out_shape=jax.ShapeDtypeStruct((t, DIFF_HEADS * LANES), BF16),
        grid=(DIFF_HEADS, t // tq),
        in_specs=[
            pl.BlockSpec(memory_space=pltpu.SMEM),
            pl.BlockSpec((tq, LANES), lambda h, i: (i, cb(COL_QA) + h)),
            pl.BlockSpec((t, LANES), lambda h, i: (0, cb(COL_KA) + h)),
            pl.BlockSpec((t, LANES), lambda h, i: (0, cb(COL_VA) + h)),
            pl.BlockSpec((4, DIFF_QK_DIM), lambda h, i: (0, 0)),
            pl.BlockSpec((1, LANES), lambda h, i: (0, 0)),
        ],
        out_specs=pl.BlockSpec((tq, LANES), lambda h, i: (i, h)),
        scratch_shapes=[
            pltpu.VMEM((2, t, LANES), BF16),
            pltpu.VMEM((nk, LANES + ONES_ROWS, tk), BF16),
            pltpu.VMEM((2, tq, LANES), BF16),
            pltpu.VMEM((2, tk, tq), F32),
            pltpu.VMEM((2, tk, tq), F32),
            pltpu.VMEM((2, 1, tq), F32),
            pltpu.VMEM((2, LANES + ONES_ROWS, tq), F32),
        ],
        compiler_params=_cparams(("arbitrary", "arbitrary")),
        name="diff_attention",
    )(slopes, proj, proj, proj, lamv, subln_g.reshape(1, LANES))


def _swa_kernel(sink_ref, qa_ref, qb_ref, k_ref, v_ref, kh_ref, vh_ref, o_ref, *, tq, slopes):
    i = pl.program_id(0)
    w = SWA_WINDOW
    low = lax.broadcasted_iota(jnp.int32, (w, LANES), 1) < SWA_HEAD_DIM
    low2 = lax.broadcasted_iota(jnp.int32, (2 * w, LANES), 1) < SWA_HEAD_DIM
    row = lax.broadcasted_iota(jnp.int32, (w, 2 * w), 0)
    col = lax.broadcasted_iota(jnp.int32, (w, 2 * w), 1)
    dist = row + w - col
    valid = jnp.logical_and(dist >= 0, dist < w)
    valid0 = jnp.logical_and(valid, jnp.logical_or(col >= w, i > 0))
    distf = dist.astype(F32)
    q_refs = (qa_ref, qb_ref)
    for r in range(tq // w):
        rs = slice(r * w, (r + 1) * w)
        ok = valid0 if r == 0 else valid
        if r == 0:
            kband = jnp.concatenate([kh_ref[...], k_ref[0:w, :]], axis=0)
            vband = jnp.concatenate([vh_ref[...], v_ref[0:w, :]], axis=0)
        else:
            kband = k_ref[(r - 1) * w:(r + 1) * w, :]
            vband = v_ref[(r - 1) * w:(r + 1) * w, :]
        kroll = pltpu.roll(kband, SWA_HEAD_DIM, axis=1)
        vroll = pltpu.roll(vband, SWA_HEAD_DIM, axis=1)
        for kv in range(2):
            kd = (jnp.where(low2, kband, kroll) if kv == 0 else jnp.where(low2, kroll, kband))
            vd = (jnp.where(low2, vband, vroll) if kv == 0 else jnp.where(low2, vroll, vband))
            kd = kd.astype(BF16)
            vd = vd.astype(BF16)
            for j in range(2):
                jj = kv * 2 + j
                q = q_refs[kv][rs, j * LANES:(j + 1) * LANES] * (SWA_HEAD_DIM ** -0.5)
                outs = []
                for half in range(2):
                    hq = 2 * jj + half
                    qh = jnp.where(low, q, 0.0) if half == 0 else jnp.where(low, 0.0, q)
                    s = _dot_nt(qh.astype(BF16), kd)
                    s = jnp.where(ok, s - slopes[hq] * distf, NEG_INF)
                    sink = sink_ref[hq]
                    m = jnp.maximum(jnp.max(s, axis=-1, keepdims=True), sink)
                    e = jnp.exp(s - m)
                    den = jnp.sum(e, axis=-1, keepdims=True) + jnp.exp(sink - m)
                    outs.append(_dot((e / den).astype(BF16), vd))
                o_ref[rs, jj * LANES:(jj + 1) * LANES] = jnp.where(low, outs[0], outs[1]).astype(BF16)


def sliding_window_attention(proj, sinks, slopes, *, tq=512):
    t = proj.shape[0]
    w = SWA_WINDOW
    qw = SWA_Q_HEADS * SWA_HEAD_DIM
    rpb = tq // w
    qblk = COL_QB // (2 * LANES)
    kblk, vblk = COL_KB // LANES, COL_VB // LANES
    return pl.pallas_call(
        functools.partial(_swa_kernel, tq=tq, slopes=tuple(float(s) for s in slopes)),
        out_shape=jax.ShapeDtypeStruct((t, qw), BF16),
        grid=(t // tq,),
        in_specs=[
            pl.BlockSpec(memory_space=pltpu.SMEM),
            pl.BlockSpec((tq, 2 * LANES), lambda i: (i, qblk)),
            pl.BlockSpec((tq, 2 * LANES), lambda i: (i, qblk + 1)),
            pl.BlockSpec((tq, LANES), lambda i: (i, kblk)),
            pl.BlockSpec((tq, LANES), lambda i: (i, vblk)),
            pl.BlockSpec((w, LANES), lambda i: (jnp.maximum(i * rpb - 1, 0), kblk)),
            pl.BlockSpec((w, LANES), lambda i: (jnp.maximum(i * rpb - 1, 0), vblk)),
        ],
        out_specs=pl.BlockSpec((tq, qw), lambda i: (i, 0)),
        compiler_params=_cparams(("parallel",)),
        name="swa_attention",
    )(sinks, proj, proj, proj, proj, proj, proj)


def _interleave(gens):
    gens = list(gens)
    results = [None] * len(gens)
    live = list(range(len(gens)))
    while live:
        still = []
        for idx in live:
            try:
                next(gens[idx])
                still.append(idx)
            except StopIteration as done:
                results[idx] = done.value
        live = still
    return results


def _seg_sum(x, low):
    s0 = jnp.sum(jnp.where(low, x, 0.0), axis=-1, keepdims=True)
    s1 = jnp.sum(jnp.where(low, 0.0, x), axis=-1, keepdims=True)
    return jnp.where(low, s0, s1)


def _stack_heads(z, low):
    zero = jnp.zeros_like(z)
    return jnp.concatenate([jnp.where(low, z, zero), jnp.where(low, zero, z)], axis=0)


def _rwkv_kernel(r_ref, k_ref, v_ref, lo_ref, pp_ref, w2_ref, a2_ref, g2_ref, y_ref,
                 h_ref, at_ref, rt_ref, bt_ref, kt_ref, bh_ref, khat_ref, vs_ref, et_ref,
                 wr_ref, u0_ref, y0_ref, arb_ref, bht_ref, kv_ref, dx_ref, ys_ref, *, tt, grp):
    i = pl.program_id(1)
    nch = tt // CHUNK
    c2 = 2 * CHUNK
    gw = grp * LANES

    @pl.when(i == 0)
    def _():
        h_ref[...] = jnp.zeros(h_ref.shape, F32)

    pp = pp_ref[...]
    w0, a0, k_k, k_a, r_k, ln_w, ln_b = (pp[0:1], pp[1:2], pp[2:3], pp[3:4], pp[4:5], pp[5:6],
                                         pp[6:7])
    r, k, v, lo = r_ref[...], k_ref[...], v_ref[...], lo_ref[...]
    z = -(w0 + _dot(jnp.tanh(lo[:, 0:LANES]).astype(BF16), w2_ref[...]))
    softplus = jnp.maximum(z, 0.0) + jnp.log(1.0 + jnp.exp(-jnp.abs(z)))
    lw = -jnp.exp(-softplus - 0.5)
    a = 1.0 / (1.0 + jnp.exp(-(a0 + _dot(lo[:, 0:2 * LANES].astype(BF16), a2_ref[...]))))
    g = _dot((1.0 / (1.0 + jnp.exp(-lo[:, LANES:4 * LANES]))).astype(BF16), g2_ref[...])

    low = lax.broadcasted_iota(jnp.int32, (tt, LANES), 1) < RWKV_HEAD_DIM
    rowc = lax.broadcasted_iota(jnp.int32, (tt, gw), 0) & (CHUNK - 1)

    def seg_sum(x):
        return jnp.concatenate(
            [_seg_sum(x[:, p * LANES:(p + 1) * LANES], low) for p in range(grp)], axis=1)

    kx = k * k_k
    kk = kx * lax.rsqrt(jnp.maximum(seg_sum(kx * kx), 1e-24))
    k2 = k * (1.0 + (a - 1.0) * k_a)
    bvec = kk * a

    cum = lw
    for s in (1, 2, 4, 8, 16, 32):
        cum = cum + jnp.where(rowc >= s, pltpu.roll(cum, s, axis=0), 0.0)
    tot = jnp.broadcast_to(cum.reshape(nch, CHUNK, gw)[:, CHUNK - 1:CHUNK, :],
                           (nch, CHUNK, gw)).reshape(tt, gw)

    einv = jnp.exp(-cum)
    erem = jnp.exp(tot - cum)
    at_ref[...] = (-kk * jnp.exp(cum - lw)).astype(BF16)
    rt_ref[...] = (r * jnp.exp(cum)).astype(BF16)
    bt_ref[...] = (bvec * einv).astype(BF16)
    kt_ref[...] = (k2 * einv).astype(BF16)
    bh_ref[...] = bvec * erem
    khat_ref[...] = k2 * erem
    vs_ref[...] = v.astype(BF16)
    et_ref[...] = jnp.exp(tot)

    low_c = lax.broadcasted_iota(jnp.int32, (CHUNK, LANES), 1) < RWKV_HEAD_DIM
    rr = lax.broadcasted_iota(jnp.int32, (CHUNK, LANES), 0)
    cc = lax.broadcasted_iota(jnp.int32, (CHUNK, LANES), 1) & (RWKV_HEAD_DIM - 1)
    lt = rr > cc
    le = rr >= cc
    eye = jnp.where(rr == cc, 1.0, 0.0)
    bd = (lax.broadcasted_iota(jnp.int32, (c2, LANES), 0) < CHUNK) == (
        lax.broadcasted_iota(jnp.int32, (c2, LANES), 1) < RWKV_HEAD_DIM)

    def prepare_chain(p, c):
        sl = pl.ds(pl.multiple_of(c * CHUNK, CHUNK), CHUNK)
        ln = slice(p * LANES, (p + 1) * LANES)
        at, rt, vs = at_ref[sl, ln], rt_ref[sl, ln], vs_ref[sl, ln]
        sb = _stack_heads(bt_ref[sl, ln], low_c)
        sk = _stack_heads(kt_ref[sl, ln], low_c)
        sv = _stack_heads(vs, low_c)
        a4 = _dot_nt(jnp.concatenate([at, rt], axis=0), jnp.concatenate([sb, sk], axis=0))
        kv = jnp.where(bd, _dot(khat_ref[sl, ln].T.astype(BF16), vs), 0.0)
        bht = bh_ref[sl, ln].T.astype(BF16)
        dx = jnp.broadcast_to(et_ref[sl, ln][0:1, :], (c2, LANES)).T
        yield
        n = jnp.where(lt, a4[:CHUNK, :LANES], 0.0)
        aak = jnp.where(lt, a4[:CHUNK, LANES:], 0.0)
        arb = jnp.where(le, a4[CHUNK:, :LANES], 0.0)
        ark = jnp.where(le, a4[CHUNK:, LANES:], 0.0)
        av = _dot(jnp.concatenate([aak, ark], axis=0).astype(BF16), sv)
        tinv = eye + n
        pw = n
        for _ in range(int(math.log2(CHUNK)) - 1):
            pb = pw.astype(BF16)
            pw = _dot(pb, _stack_heads(pb, low_c))
            yield
            tinv = tinv + _dot(pw.astype(BF16), _stack_heads(tinv.astype(BF16), low_c))
            yield
        rhs = jnp.concatenate([_stack_heads(at, low_c),
                               _stack_heads(av[:CHUNK].astype(BF16), low_c)], axis=1)
        wu = _dot(tinv.astype(BF16), rhs)
        yield
        return (jnp.concatenate([wu[:, :LANES].astype(BF16), rt], axis=0), wu[:, LANES:],
                av[CHUNK:], arb.astype(BF16), bht, kv, dx)

    def prepare(cc, carry):
        chains = [(p, cc * PREP_UNROLL + u) for u in range(PREP_UNROLL) for p in range(grp)]
        results = _interleave([prepare_chain(p, c) for p, c in chains])
        for (p, c), vals in zip(chains, results):
            for ref, val in zip((wr_ref, u0_ref, y0_ref, arb_ref, bht_ref, kv_ref, dx_ref), vals):
                ref[p, c] = val
        return carry

    lax.fori_loop(0, nch // PREP_UNROLL, prepare, 0)

    def advance_chain(p, c):
        h = h_ref[p]
        wh = _dot(wr_ref[p, c], h.astype(BF16))
        yield
        u = (wh[:CHUNK] + u0_ref[p, c]).astype(BF16)
        y = wh[CHUNK:] + _dot(arb_ref[p, c], _stack_heads(u, low_c)) + y0_ref[p, c]
        h = dx_ref[p, c] * h + jnp.where(bd, _dot(bht_ref[p, c], u), 0.0) + kv_ref[p, c]
        yield
        return y, h

    def advance(c, carry):
        sl = pl.ds(pl.multiple_of(c * CHUNK, CHUNK), CHUNK)
        results = _interleave([advance_chain(p, c) for p in range(grp)])
        for p, (y, h) in enumerate(results):
            ys_ref[sl, p * LANES:(p + 1) * LANES] = y
            h_ref[p] = h
        return carry

    lax.fori_loop(0, nch, advance, 0)

    y = ys_ref[...]
    mean = seg_sum(y) * (1.0 / RWKV_HEAD_DIM)
    yc = y - mean
    var = seg_sum(yc * yc) * (1.0 / RWKV_HEAD_DIM)
    yn = yc * lax.rsqrt(var + RWKV_GN_EPS) * ln_w + ln_b
    bonus = seg_sum(r * k2 * r_k) * v
    y_ref[...] = ((yn + bonus) * g).astype(BF16)


def rwkv7_mix(proj, pp, w2p, a2p, g2p, layer, *, tt=512, grp=RWKV_GROUP):
    t = proj.shape[0]
    nch = tt // CHUNK
    c2 = 2 * CHUNK
    gw = grp * LANES
    cb = lambda col: col // gw
    lb = COL_LORA // LORA_COLS
    in_specs = [
        pl.BlockSpec((tt, gw), lambda p, i: (i, cb(COL_R) + p)),
        pl.BlockSpec((tt, gw), lambda p, i: (i, cb(COL_K) + p)),
        pl.BlockSpec((tt, gw), lambda p, i: (i, cb(COL_V) + p)),
        pl.BlockSpec((tt, LORA_COLS), lambda p, i: (i, lb)),
        pl.BlockSpec((None, SUBLANES, gw), lambda p, i: (layer, 0, p)),
        pl.BlockSpec((None, LANES, gw), lambda p, i: (layer, 0, p)),
        pl.BlockSpec((None, 2 * LANES, gw), lambda p, i: (layer, 0, p)),
        pl.BlockSpec((None, 3 * LANES, gw), lambda p, i: (layer, 0, p)),
    ]
    scratch = [
        pltpu.VMEM((grp, c2, LANES), F32),
        pltpu.VMEM((tt, gw), BF16),
        pltpu.VMEM((tt, gw), BF16),
        pltpu.VMEM((tt, gw), BF16),
        pltpu.VMEM((tt, gw), BF16),
        pltpu.VMEM((tt, gw), F32),
        pltpu.VMEM((tt, gw), F32),
        pltpu.VMEM((tt, gw), BF16),
        pltpu.VMEM((tt, gw), F32),
        pltpu.VMEM((grp, nch, c2, LANES), BF16),
        pltpu.VMEM((grp, nch, CHUNK, LANES), F32),
        pltpu.VMEM((grp, nch, CHUNK, LANES), F32),
        pltpu.VMEM((grp, nch, CHUNK, LANES), BF16),
        pltpu.VMEM((grp, nch, LANES, CHUNK), BF16),
        pltpu.VMEM((grp, nch, c2, LANES), F32),
        pltpu.VMEM((grp, nch, c2, LANES), F32),
        pltpu.VMEM((tt, gw), F32),
    ]
    return pl.pallas_call(
        functools.partial(_rwkv_kernel, tt=tt, grp=grp),
        out_shape=jax.ShapeDtypeStruct((t, RWKV_WIDTH), BF16),
        grid=(RWKV_PAIRS // grp, t // tt),
        in_specs=in_specs,
        out_specs=pl.BlockSpec((tt, gw), lambda p, i: (i, p)),
        scratch_shapes=scratch,
        compiler_params=_cparams(("arbitrary", "arbitrary")),
        name="rwkv7_mix",
    )(proj, proj, proj, proj, pp, w2p, a2p, g2p)


def _alibi_slopes():
    idx = np.arange(1, ALIBI_HEADS + 1, dtype=np.float32)
    m = np.exp2(-8.0 * idx / ALIBI_HEADS).astype(np.float32)
    diff_idx = np.arange(2, ALIBI_HEADS, 3)
    swa_idx = np.setdiff1d(np.arange(ALIBI_HEADS), diff_idx)
    return m[diff_idx], m[swa_idx]


def _pad_rows(w, before, total):
    return jnp.pad(w, ((0, 0), (before, total - before - w.shape[1]), (0, 0)))


def kernel(x, attn_norm_g, w_in, diff_lambda, diff_subln_g, swa_sinks, rwkv_mu, rwkv_w0, rwkv_w2,
           rwkv_a0, rwkv_a2, rwkv_g2, rwkv_k_k, rwkv_k_a, rwkv_r_k, rwkv_ln_w, rwkv_ln_b, w_out,
           ffn_norm_g, w_gate_up, w_down, final_norm_g):
    b, t, d = x.shape
    depth = w_in.shape[0]
    diff_slopes, swa_slopes = _alibi_slopes()
    diff_slopes = jnp.asarray(diff_slopes)

    pad_cols = lambda w: jnp.pad(w, [(0, 0)] * (w.ndim - 1) + [(PROJ_PAD, PROJ_COLS - PROJ_PAD - IN_COLS)])
    w_in_b = pad_cols(w_in.astype(BF16))
    w_out_b = w_out.astype(BF16)
    w_gu_b = w_gate_up.astype(BF16)
    w_down_b = w_down.astype(BF16)
    mu_cols = pad_cols(jnp.pad(rwkv_mu, ((0, 0), (IN_COLS - rwkv_mu.shape[1], 0)))).reshape(
        depth, 1, PROJ_COLS)
    w2_b = _pad_rows(rwkv_w2, 0, LANES).astype(BF16)
    a2_b = _pad_rows(rwkv_a2, DECAY_LORA, 2 * LANES).astype(BF16)
    g2_b = _pad_rows(rwkv_g2, DECAY_LORA + ICLR_LORA - LANES, 3 * LANES).astype(BF16)
    pp = jnp.stack([rwkv_w0, rwkv_a0, rwkv_k_k, rwkv_k_a, rwkv_r_k.reshape(depth, RWKV_WIDTH),
                    rwkv_ln_w, rwkv_ln_b, jnp.zeros_like(rwkv_w0)], axis=1)
    attn_g = attn_norm_g.reshape(depth, 1, d)
    ffn_g = ffn_norm_g.reshape(depth, 1, d)

    outs = []
    for bi in range(b):
        xb = x[bi]
        for l in range(depth):
            lambda_init = 0.8 - 0.6 * math.exp(-0.3 * l)
            proj = norm_matmul(xb, attn_g, w_in_b, mu_cols, l)
            ya = diff_attention(proj, diff_lambda[l], diff_subln_g[l], diff_slopes,
                                lambda_init=lambda_init)
            yb = sliding_window_attention(proj, swa_sinks[l], swa_slopes)
            yc = rwkv7_mix(proj, pp, w2_b, a2_b, g2_b, l)
            xb = matmul_residual([ya, yb, yc], w_out_b, l, xb, tn=1024, name="out_proj")
            hid = ffn_up(xb, ffn_g, w_gu_b, l)
            xb = matmul_residual([hid], w_down_b, l, xb, name="ffn_down")
        outs.append(rmsnorm(xb, final_norm_g))
    return jnp.stack(outs, axis=0)
```

```python
import functools
import math

import numpy as np
import jax
import jax.numpy as jnp
from jax import lax
from jax.experimental import pallas as pl
from jax.experimental.pallas import tpu as pltpu

F32 = jnp.float32
BF16 = jnp.bfloat16

D_MODEL = 2048
LANES = 128
SUBLANES = 8
DIFF_HEADS = 4
DIFF_QK_DIM = 64
SWA_Q_HEADS = 8
SWA_HEAD_DIM = 64
SWA_WINDOW = 128
RWKV_HEADS = 16
RWKV_HEAD_DIM = 64
RWKV_WIDTH = RWKV_HEADS * RWKV_HEAD_DIM
RWKV_PAIRS = RWKV_WIDTH // LANES
DECAY_LORA = 96
ICLR_LORA = 96
GATE_LORA = 256
FFN_HIDDEN = 5632
ALIBI_HEADS = 12
NORM_EPS = 1e-5
RWKV_GN_EPS = 64e-5
NEG_INF = -1e30
CHUNK = 64
RWKV_GROUP = 4
CHUNK_STAGGER = 3

IN_COLS = 5824
COL_QA, COL_KA, COL_VA = 0, 512, 1024
COL_QB, COL_KB, COL_VB = 1536, 2048, 2176
COL_R, COL_K, COL_V = 2304, 3328, 4352
COL_LORA = 5376
LORA_COLS = 512
LORA_VALID = DECAY_LORA + ICLR_LORA + GATE_LORA
RWKV_BLOCK = 2 * LANES
NORM_CHUNKS = 4

VMEM_LIMIT = 56 * 1024 * 1024


def _cparams(sem):
    return pltpu.CompilerParams(dimension_semantics=sem, vmem_limit_bytes=VMEM_LIMIT)


def _dot(a, b):
    return jnp.dot(a, b, preferred_element_type=F32)


def _dot_nt(a, b):
    return lax.dot_general(a, b, (((1,), (1,)), ((), ())), preferred_element_type=F32)


def _norm_rows(x, g):
    ms = jnp.mean(x * x, axis=-1, keepdims=True)
    return (x * lax.rsqrt(ms + NORM_EPS) * g).astype(BF16)


def _norm_then_dots(x_ref, g_ref, h_ref, weights):
    tm = x_ref.shape[0]
    cm = tm // NORM_CHUNKS
    g = g_ref[...]
    parts = []
    for c in range(NORM_CHUNKS):
        rows = slice(c * cm, (c + 1) * cm)
        h = _norm_rows(x_ref[rows, :], g)
        h_ref[rows, :] = h
        parts.append([_dot(h, w) for w in weights])
    return [jnp.concatenate([p[k] for p in parts], axis=0) for k in range(len(weights))]


def _norm_mm_kernel(x_ref, g_ref, w_ref, mu_ref, o_ref, h_ref, last_ref):
    i = pl.program_id(0)
    j = pl.program_id(1)
    w = w_ref[...].astype(BF16)

    def finish(acc):
        rowi = lax.broadcasted_iota(jnp.int32, acc.shape, 0)
        first = jnp.where(i > 0, last_ref[j, SUBLANES - 1:SUBLANES, :], 0.0)
        prev = jnp.where(rowi == 0, first, pltpu.roll(acc, 1, axis=0))
        last_ref[j] = acc[acc.shape[0] - SUBLANES:, :]
        o_ref[...] = acc + (prev - acc) * mu_ref[...]

    @pl.when(j == 0)
    def _():
        finish(_norm_then_dots(x_ref, g_ref, h_ref, [w])[0])

    @pl.when(j > 0)
    def _():
        finish(_dot(h_ref[...], w))


def norm_matmul(x, g, w, mu, layer, *, tm=1024, tn=1024):
    t, d = x.shape
    n = w.shape[2]
    nj = pl.cdiv(n, tn)
    return pl.pallas_call(
        _norm_mm_kernel,
        out_shape=jax.ShapeDtypeStruct((t, n), F32),
        grid=(t // tm, nj),
        in_specs=[
            pl.BlockSpec((tm, d), lambda i, j: (i, 0)),
            pl.BlockSpec((None, 1, d), lambda i, j: (layer, 0, 0)),
            pl.BlockSpec((None, d, tn), lambda i, j: (layer, 0, j)),
            pl.BlockSpec((None, 1, tn), lambda i, j: (layer, 0, j)),
        ],
        out_specs=pl.BlockSpec((tm, tn), lambda i, j: (i, j)),
        scratch_shapes=[pltpu.VMEM((tm, d), BF16), pltpu.VMEM((nj, SUBLANES, tn), F32)],
        compiler_params=_cparams(("arbitrary", "arbitrary")),
        name="norm_in_proj",
    )(x, g, w, mu)


def _ffn_up_kernel(x_ref, g_ref, wg_ref, wu_ref, o_ref, h_ref):
    j = pl.program_id(1)
    wg = wg_ref[...].astype(BF16)
    wu = wu_ref[...].astype(BF16)

    def finish(gate, up):
        o_ref[...] = (gate / (1.0 + jnp.exp(-gate)) * up).astype(BF16)

    @pl.when(j == 0)
    def _():
        finish(*_norm_then_dots(x_ref, g_ref, h_ref, [wg, wu]))

    @pl.when(j > 0)
    def _():
        h = h_ref[...]
        finish(_dot(h, wg), _dot(h, wu))


def ffn_up(x, g, w_gate_up, layer, *, tm=1024, tn=512):
    t, d = x.shape
    hid = w_gate_up.shape[2] // 2
    nj = hid // tn
    return pl.pallas_call(
        _ffn_up_kernel,
        out_shape=jax.ShapeDtypeStruct((t, hid), BF16),
        grid=(t // tm, nj),
        in_specs=[
            pl.BlockSpec((tm, d), lambda i, j: (i, 0)),
            pl.BlockSpec((None, 1, d), lambda i, j: (layer, 0, 0)),
            pl.BlockSpec((None, d, tn), lambda i, j: (layer, 0, j)),
            pl.BlockSpec((None, d, tn), lambda i, j: (layer, 0, j + nj)),
        ],
        out_specs=pl.BlockSpec((tm, tn), lambda i, j: (i, j)),
        scratch_shapes=[pltpu.VMEM((tm, d), BF16)],
        compiler_params=_cparams(("parallel", "arbitrary")),
        name="ffn_up",
    )(x, g, w_gate_up, w_gate_up)


def _mm_res_kernel(*refs, splits):
    n = len(splits)
    a_refs, (w_ref, x_ref, o_ref) = refs[:n], refs[n:]
    acc = x_ref[...]
    off = 0
    for a_ref, k in zip(a_refs, splits):
        acc = acc + _dot(a_ref[...], w_ref[off:off + k, :].astype(BF16))
        off += k
    o_ref[...] = acc


def matmul_residual(a_list, w, layer, x, *, tm=1024, tn=512, name):
    t, n = x.shape
    splits = tuple(a.shape[1] for a in a_list)
    k = sum(splits)
    in_specs = [pl.BlockSpec((tm, s), lambda i, j: (i, 0)) for s in splits]
    in_specs += [
        pl.BlockSpec((None, k, tn), lambda i, j: (layer, 0, j)),
        pl.BlockSpec((tm, tn), lambda i, j: (i, j)),
    ]
    return pl.pallas_call(
        functools.partial(_mm_res_kernel, splits=splits),
        out_shape=jax.ShapeDtypeStruct((t, n), F32),
        grid=(t // tm, n // tn),
        in_specs=in_specs,
        out_specs=pl.BlockSpec((tm, tn), lambda i, j: (i, j)),
        compiler_params=_cparams(("parallel", "arbitrary")),
        name=name,
    )(*a_list, w, x)


def _rmsnorm_kernel(x_ref, g_ref, o_ref):
    x = x_ref[...]
    ms = jnp.mean(x * x, axis=-1, keepdims=True)
    o_ref[...] = x * lax.rsqrt(ms + NORM_EPS) * g_ref[...]


def rmsnorm(x, g, *, tm=512):
    t, d = x.shape
    return pl.pallas_call(
        _rmsnorm_kernel,
        out_shape=jax.ShapeDtypeStruct((t, d), F32),
        grid=(t // tm,),
        in_specs=[pl.BlockSpec((tm, d), lambda i: (i, 0)), pl.BlockSpec((1, d), lambda i: (0, 0))],
        out_specs=pl.BlockSpec((tm, d), lambda i: (i, 0)),
        compiler_params=_cparams(("parallel",)),
        name="final_rmsnorm",
    )(x, g.reshape(1, d))


LOG2E = 1.4426950408889634
POS_SPLIT = 128
ONES_ROWS = 16


def _pos_lanes(lane, base, cols):
    out = jnp.zeros(lane.shape, F32)
    for idx, col in enumerate(cols):
        out = jnp.where(lane == base + idx, col, out)
    return out


def _diff_attn_kernel(slope_ref, q_ref, k_ref, v_ref, lam_ref, g_ref, o_ref,
                      kb_ref, vt_ref, qs_ref, sa_ref, sb_ref, m_ref, acc_ref, *, tq, tk, nk,
                      lambda_init):
    h = pl.program_id(0)
    i = pl.program_id(1)
    kpq = tq // tk
    beta = jnp.full((1, LANES), slope_ref[h] * LOG2E, F32)
    b1 = beta.astype(BF16).astype(F32)
    b2 = (beta - b1).astype(BF16).astype(F32)
    b3 = (beta - b1 - b2).astype(BF16).astype(F32)
    own = [lambda lane, c=c: (lane < DIFF_QK_DIM) == (c == 0) for c in range(2)]
    pos_base = [DIFF_QK_DIM, 0]

    @pl.when(i == 0)
    def _():
        lane = lax.broadcasted_iota(jnp.int32, (tk, LANES), 1)
        krel = lax.broadcasted_iota(jnp.int32, (tk, LANES), 0)
        khi = ((krel // POS_SPLIT) * POS_SPLIT).astype(F32)
        klo = (krel % POS_SPLIT).astype(F32)
        kcols = []
        for b in (b1, b2, b3):
            kcols += [-b * POS_SPLIT, -b, khi, klo]
        posk = [_pos_lanes(lane, pos_base[c], kcols) for c in range(2)]
        ones = jnp.where(lax.broadcasted_iota(jnp.int32, (ONES_ROWS, tk), 0) == 0, 1.0, 0.0)

        def stage(j, carry):
            rows = pl.ds(pl.multiple_of(j * tk, tk), tk)
            k = k_ref[rows, :]
            for c in range(2):
                kb_ref[c, rows, :] = jnp.where(own[c](lane), k, posk[c]).astype(BF16)
            vt_ref[j] = jnp.concatenate([v_ref[rows, :].T, ones], axis=0).astype(BF16)
            return carry

        lax.fori_loop(0, nk, stage, 0)

    q = q_ref[...] * (DIFF_QK_DIM ** -0.5 * LOG2E)
    lane = lax.broadcasted_iota(jnp.int32, (tq, LANES), 1)
    qrel = lax.broadcasted_iota(jnp.int32, (tq, LANES), 0)
    qhi = (qrel // POS_SPLIT).astype(F32)
    qlo = (qrel % POS_SPLIT).astype(F32)
    qcols = []
    for b in (b1, b2, b3):
        qcols += [qhi, qlo, b, b]
    for c in range(2):
        qs_ref[c] = jnp.where(own[c](lane), q, _pos_lanes(lane, pos_base[c], qcols)).astype(BF16)
    m_ref[...] = jnp.full(m_ref.shape, NEG_INF, F32)
    acc_ref[...] = jnp.zeros(acc_ref.shape, F32)

    def scores(j, dst_ref, qlo):
        rows = pl.ds(pl.multiple_of(j * tk, tk), tk)

        def chain(c):
            s = _dot_nt(kb_ref[c, rows, :], qs_ref[c, qlo:, :])
            yield
            return s

        return [chain(c) for c in range(2)], dst_ref

    def consume(src_ref, j, qlo, masked):
        vt = vt_ref[j]
        c0 = (slope_ref[h] * LOG2E) * (j * tk - i * tq).astype(F32)

        def chain(c):
            s = src_ref[c, :, qlo:]
            if masked:
                key = lax.broadcasted_iota(jnp.int32, s.shape, 0)
                qry = lax.broadcasted_iota(jnp.int32, s.shape, 1)
                s = jnp.where(qry >= key, s, NEG_INF)
            m_prev = m_ref[c, :, qlo:]
            m_new = jnp.maximum(m_prev, jnp.max(s, axis=0, keepdims=True) + c0)
            yield
            p = jnp.exp2(s - (m_new - c0)).astype(BF16)
            alpha = jnp.exp2(m_prev - m_new)
            yield
            return alpha * acc_ref[c, :, qlo:] + _dot(vt, p), m_new

        return [chain(c) for c in range(2)]

    def step(score_args, consume_args):
        chains = consume(*consume_args)
        dst, dst_qlo = None, 0
        if score_args is not None:
            score_chains, dst = scores(*score_args)
            dst_qlo = score_args[2]
            chains = chains + score_chains
        results = _interleave(chains)
        qlo = consume_args[2]
        for c in range(2):
            acc_new, m_new = results[c]
            acc_ref[c, :, qlo:] = acc_new
            m_ref[c, :, qlo:] = m_new
            if dst is not None:
                dst[c, :, dst_qlo:] = results[2 + c]

    for c, s in enumerate(_interleave(scores(0, sa_ref, 0)[0])):
        sa_ref[c] = s

    def body(jj, carry):
        j = jj * kpq
        step((j + 1, sb_ref, 0), (sa_ref, j, 0, False))
        step((j + 2, sa_ref, 0), (sb_ref, j + 1, 0, False))
        return carry

    lax.fori_loop(0, i, body, 0)
    step((i * kpq + 1, sb_ref, tk), (sa_ref, i * kpq, 0, True))
    step(None, (sb_ref, i * kpq + 1, tk, True))

    lamv = lam_ref[...]
    s01 = jnp.sum(lamv[0:1] * lamv[1:2], axis=-1, keepdims=True)
    s23 = jnp.sum(lamv[2:3] * lamv[3:4], axis=-1, keepdims=True)
    lam = jnp.exp(s01) - jnp.exp(s23) + lambda_init
    a0, a1 = acc_ref[0], acc_ref[1]
    ot = a0[:LANES] / a0[LANES:LANES + 1] - lam * (a1[:LANES] / a1[LANES:LANES + 1])
    o = ot.T
    o = o * lax.rsqrt(jnp.mean(o * o, axis=-1, keepdims=True) + NORM_EPS)
    o_ref[...] = (o * g_ref[...] * (1.0 - lambda_init)).astype(BF16)


def diff_attention(proj, lamv, subln_g, slopes, *, lambda_init, tq=1024, tk=512):
    t = proj.shape[0]
    nk = t // tk
    assert tq == 2 * tk and t % tq == 0
    cb = lambda col: col // LANES
    return pl.pallas_call(
        functools.partial(_diff_attn_kernel, tq=tq, tk=tk, nk=nk, lambda_init=lambda_init),
        out_shape=jax.ShapeDtypeStruct((t, DIFF_HEADS * LANES), BF16),
        grid=(DIFF_HEADS, t // tq),
        in_specs=[
            pl.BlockSpec(memory_space=pltpu.SMEM),
            pl.BlockSpec((tq, LANES), lambda h, i: (i, cb(COL_QA) + h)),
            pl.BlockSpec((t, LANES), lambda h, i: (0, cb(COL_KA) + h)),
            pl.BlockSpec((t, LANES), lambda h, i: (0, cb(COL_VA) + h)),
            pl.BlockSpec((4, DIFF_QK_DIM), lambda h, i: (0, 0)),
            pl.BlockSpec((1, LANES), lambda h, i: (0, 0)),
        ],
        out_specs=pl.BlockSpec((tq, LANES), lambda h, i: (i, h)),
        scratch_shapes=[
            pltpu.VMEM((2, t, LANES), BF16),
            pltpu.VMEM((nk, LANES + ONES_ROWS, tk), BF16),
            pltpu.VMEM((2, tq, LANES), BF16),
            pltpu.VMEM((2, tk, tq), F32),
            pltpu.VMEM((2, tk, tq), F32),
            pltpu.VMEM((2, 1, tq), F32),
            pltpu.VMEM((2, LANES + ONES_ROWS, tq), F32),
        ],
        compiler_params=_cparams(("arbitrary", "arbitrary")),
        name="diff_attention",
    )(slopes, proj, proj, proj, lamv, subln_g.reshape(1, LANES))


def _swa_kernel(sink_ref, qa_ref, qb_ref, k_ref, v_ref, kh_ref, vh_ref, o_ref, *, tq, slopes):
    i = pl.program_id(0)
    w = SWA_WINDOW
    low = lax.broadcasted_iota(jnp.int32, (w, LANES), 1) < SWA_HEAD_DIM
    low2 = lax.broadcasted_iota(jnp.int32, (2 * w, LANES), 1) < SWA_HEAD_DIM
    row = lax.broadcasted_iota(jnp.int32, (w, 2 * w), 0)
    col = lax.broadcasted_iota(jnp.int32, (w, 2 * w), 1)
    dist = row + w - col
    valid = jnp.logical_and(dist >= 0, dist < w)
    valid0 = jnp.logical_and(valid, jnp.logical_or(col >= w, i > 0))
    distf = dist.astype(F32)
    q_refs = (qa_ref, qb_ref)
    for r in range(tq // w):
        rs = slice(r * w, (r + 1) * w)
        ok = valid0 if r == 0 else valid
        if r == 0:
            kband = jnp.concatenate([kh_ref[...], k_ref[0:w, :]], axis=0)
            vband = jnp.concatenate([vh_ref[...], v_ref[0:w, :]], axis=0)
        else:
            kband = k_ref[(r - 1) * w:(r + 1) * w, :]
            vband = v_ref[(r - 1) * w:(r + 1) * w, :]
        kroll = pltpu.roll(kband, SWA_HEAD_DIM, axis=1)
        vroll = pltpu.roll(vband, SWA_HEAD_DIM, axis=1)
        for kv in range(2):
            kd = (jnp.where(low2, kband, kroll) if kv == 0 else jnp.where(low2, kroll, kband))
            vd = (jnp.where(low2, vband, vroll) if kv == 0 else jnp.where(low2, vroll, vband))
            kd = kd.astype(BF16)
            vd = vd.astype(BF16)
            for j in range(2):
                jj = kv * 2 + j
                q = q_refs[kv][rs, j * LANES:(j + 1) * LANES] * (SWA_HEAD_DIM ** -0.5)
                outs = []
                for half in range(2):
                    hq = 2 * jj + half
                    qh = jnp.where(low, q, 0.0) if half == 0 else jnp.where(low, 0.0, q)
                    s = _dot_nt(qh.astype(BF16), kd)
                    s = jnp.where(ok, s - slopes[hq] * distf, NEG_INF)
                    sink = sink_ref[hq]
                    m = jnp.maximum(jnp.max(s, axis=-1, keepdims=True), sink)
                    e = jnp.exp(s - m)
                    den = jnp.sum(e, axis=-1, keepdims=True) + jnp.exp(sink - m)
                    outs.append(_dot((e / den).astype(BF16), vd))
                o_ref[rs, jj * LANES:(jj + 1) * LANES] = jnp.where(low, outs[0], outs[1]).astype(BF16)


def sliding_window_attention(proj, sinks, slopes, *, tq=512):
    t = proj.shape[0]
    w = SWA_WINDOW
    qw = SWA_Q_HEADS * SWA_HEAD_DIM
    rpb = tq // w
    qblk = COL_QB // (2 * LANES)
    kblk, vblk = COL_KB // LANES, COL_VB // LANES
    return pl.pallas_call(
        functools.partial(_swa_kernel, tq=tq, slopes=tuple(float(s) for s in slopes)),
        out_shape=jax.ShapeDtypeStruct((t, qw), BF16),
        grid=(t // tq,),
        in_specs=[
            pl.BlockSpec(memory_space=pltpu.SMEM),
            pl.BlockSpec((tq, 2 * LANES), lambda i: (i, qblk)),
            pl.BlockSpec((tq, 2 * LANES), lambda i: (i, qblk + 1)),
            pl.BlockSpec((tq, LANES), lambda i: (i, kblk)),
            pl.BlockSpec((tq, LANES), lambda i: (i, vblk)),
            pl.BlockSpec((w, LANES), lambda i: (jnp.maximum(i * rpb - 1, 0), kblk)),
            pl.BlockSpec((w, LANES), lambda i: (jnp.maximum(i * rpb - 1, 0), vblk)),
        ],
        out_specs=pl.BlockSpec((tq, qw), lambda i: (i, 0)),
        compiler_params=_cparams(("parallel",)),
        name="swa_attention",
    )(sinks, proj, proj, proj, proj, proj, proj)


def _interleave(gens):
    gens = list(gens)
    results = [None] * len(gens)
    live = list(range(len(gens)))
    while live:
        still = []
        for idx in live:
            try:
                next(gens[idx])
                still.append(idx)
            except StopIteration as done:
                results[idx] = done.value
        live = still
    return results


def _seg_sum(x, low):
    s0 = jnp.sum(jnp.where(low, x, 0.0), axis=-1, keepdims=True)
    s1 = jnp.sum(jnp.where(low, 0.0, x), axis=-1, keepdims=True)
    return jnp.where(low, s0, s1)


def _stack_heads(z, low):
    zero = jnp.zeros_like(z)
    return jnp.concatenate([jnp.where(low, z, zero), jnp.where(low, zero, z)], axis=0)


def _round_robin(subs, results):
    live = list(range(len(subs)))
    while live:
        still = []
        for idx in live:
            try:
                next(subs[idx])
                still.append(idx)
            except StopIteration as done:
                results[idx] = done.value
        live = still
        yield


def _rwkv_kernel(*refs, tt, grp):
    nb = grp * LANES // RWKV_BLOCK
    r_refs, k_refs, v_refs = refs[0:nb], refs[nb:2 * nb], refs[2 * nb:3 * nb]
    lo_refs = refs[3 * nb:3 * nb + 4]
    _rwkv_body(r_refs, k_refs, v_refs, lo_refs, *refs[3 * nb + 4:], tt=tt, grp=grp)


def _rwkv_body(r_refs, k_refs, v_refs, lo_refs, pp_ref, w2_ref, a2_ref, g2_ref, y_ref, h_ref, *,
               tt, grp):
    i = pl.program_id(1)
    nch = tt // CHUNK
    c2 = 2 * CHUNK
    gw = grp * LANES

    @pl.when(i == 0)
    def _():
        h_ref[...] = jnp.zeros(h_ref.shape, F32)

    pp = pp_ref[...]
    w0, a0, k_k, k_a, r_k, ln_w, ln_b = (pp[0:1], pp[1:2], pp[2:3], pp[3:4], pp[4:5], pp[5:6],
                                         pp[6:7])
    low_c = lax.broadcasted_iota(jnp.int32, (CHUNK, LANES), 1) < RWKV_HEAD_DIM
    rr = lax.broadcasted_iota(jnp.int32, (CHUNK, LANES), 0)
    cc = lax.broadcasted_iota(jnp.int32, (CHUNK, LANES), 1) & (RWKV_HEAD_DIM - 1)
    lt = rr > cc
    le = rr >= cc
    eye = jnp.where(rr == cc, 1.0, 0.0)
    bd = (lax.broadcasted_iota(jnp.int32, (c2, LANES), 0) < CHUNK) == (
        lax.broadcasted_iota(jnp.int32, (c2, LANES), 1) < RWKV_HEAD_DIM)
    rowc = lax.broadcasted_iota(jnp.int32, (CHUNK, gw), 0)
    lo_tail_ok = lax.broadcasted_iota(jnp.int32, (CHUNK, LANES), 1) < LORA_VALID - 3 * LANES

    def seg_sum(x):
        return jnp.concatenate(
            [_seg_sum(x[:, p * LANES:(p + 1) * LANES], low_c) for p in range(grp)], axis=1)

    def prepare_chain(at, rt, bt, kt, vs, bh, khat, etot):
        sb = _stack_heads(bt, low_c)
        sk = _stack_heads(kt, low_c)
        sv = _stack_heads(vs, low_c)
        a4 = _dot_nt(jnp.concatenate([at, rt], axis=0), jnp.concatenate([sb, sk], axis=0))
        kv = jnp.where(bd, _dot(khat.T.astype(BF16), vs), 0.0)
        bht = bh.T.astype(BF16)
        dx = jnp.broadcast_to(etot, (c2, LANES)).T
        yield
        n = jnp.where(lt, a4[:CHUNK, :LANES], 0.0)
        aak = jnp.where(lt, a4[:CHUNK, LANES:], 0.0)
        arb = jnp.where(le, a4[CHUNK:, :LANES], 0.0)
        ark = jnp.where(le, a4[CHUNK:, LANES:], 0.0)
        av = _dot(jnp.concatenate([aak, ark], axis=0).astype(BF16), sv)
        tinv = eye + n
        pw = n
        for _ in range(int(math.log2(CHUNK)) - 1):
            pb = pw.astype(BF16)
            pw = _dot(pb, _stack_heads(pb, low_c))
            yield
            tinv = tinv + _dot(pw.astype(BF16), _stack_heads(tinv.astype(BF16), low_c))
            yield
        rhs = jnp.concatenate([_stack_heads(at, low_c),
                               _stack_heads(av[:CHUNK].astype(BF16), low_c)], axis=1)
        wu = _dot(tinv.astype(BF16), rhs)
        yield
        return (jnp.concatenate([wu[:, :LANES].astype(BF16), rt], axis=0), wu[:, LANES:],
                av[CHUNK:], arb.astype(BF16), bht, kv, dx)

    def chunk_chain(c):
        rows = slice(c * CHUNK, (c + 1) * CHUNK)
        lo0, lo1, lo2 = (lo_refs[n][rows, :] for n in range(3))
        lo3 = jnp.where(lo_tail_ok, lo_refs[3][rows, :], 0.0)
        z = -(w0 + _dot(jnp.tanh(lo0).astype(BF16), w2_ref[...]))
        a_pre = a0 + _dot(jnp.concatenate([lo0, lo1], axis=1).astype(BF16), a2_ref[...])
        gate_in = jnp.concatenate([lo1, lo2, lo3], axis=1)
        g = _dot((1.0 / (1.0 + jnp.exp(-gate_in))).astype(BF16), g2_ref[...])
        yield
        r, k, v = (jnp.concatenate([ref[rows, :] for ref in refs_], axis=1)
                   for refs_ in (r_refs, k_refs, v_refs))
        softplus = jnp.maximum(z, 0.0) + jnp.log(1.0 + jnp.exp(-jnp.abs(z)))
        lw = -jnp.exp(-softplus - 0.5)
        a = 1.0 / (1.0 + jnp.exp(-a_pre))
        kx = k * k_k
        kk = kx * lax.rsqrt(jnp.maximum(seg_sum(kx * kx), 1e-24))
        k2 = k * (1.0 + (a - 1.0) * k_a)
        bvec = kk * a
        cum = lw
        for s in (1, 2, 4, 8, 16, 32):
            cum = cum + jnp.where(rowc >= s, pltpu.roll(cum, s, axis=0), 0.0)
        tot = cum[CHUNK - 1:CHUNK, :]
        einv = jnp.exp(-cum)
        erem = jnp.exp(tot - cum)
        at = (-kk * jnp.exp(cum - lw)).astype(BF16)
        rt = (r * jnp.exp(cum)).astype(BF16)
        bt = (bvec * einv).astype(BF16)
        kt = (k2 * einv).astype(BF16)
        bh = bvec * erem
        khat = k2 * erem
        vs = v.astype(BF16)
        etot = jnp.exp(tot)
        bonus = seg_sum(r * k2 * r_k) * v
        yield
        pairs = [None] * grp
        subs = []
        for p in range(grp):
            ln = slice(p * LANES, (p + 1) * LANES)
            subs.append(prepare_chain(at[:, ln], rt[:, ln], bt[:, ln], kt[:, ln], vs[:, ln],
                                      bh[:, ln], khat[:, ln], etot[:, ln]))
        yield from _round_robin(subs, pairs)
        return pairs, g, bonus

    hs = [h_ref[p] for p in range(grp)]

    def advance_chain(c, pairs, g, bonus):
        rows = slice(c * CHUNK, (c + 1) * CHUNK)
        whs = [_dot(pairs[p][0], hs[p].astype(BF16)) for p in range(grp)]
        yield
        ys = []
        for p in range(grp):
            _, u0, y0, arb, bht, kv, dx = pairs[p]
            u = (whs[p][:CHUNK] + u0).astype(BF16)
            ys.append(whs[p][CHUNK:] + _dot(arb, _stack_heads(u, low_c)) + y0)
            hs[p] = dx * hs[p] + jnp.where(bd, _dot(bht, u), 0.0) + kv
        yield
        y = jnp.concatenate(ys, axis=1)
        mean = seg_sum(y) * (1.0 / RWKV_HEAD_DIM)
        yc = y - mean
        var = seg_sum(yc * yc) * (1.0 / RWKV_HEAD_DIM)
        yn = yc * lax.rsqrt(var + RWKV_GN_EPS) * ln_w + ln_b
        y_ref[rows, :] = ((yn + bonus) * g).astype(BF16)

    chains = [chunk_chain(c) for c in range(nch)]
    ready = {}
    active, started, rnd = [], 0, 0
    adv, adv_next = None, 0
    while adv_next < nch:
        if started < nch and rnd % CHUNK_STAGGER == 0:
            active.append(started)
            started += 1
        for c in list(active):
            try:
                next(chains[c])
            except StopIteration as done:
                ready[c] = done.value
                active.remove(c)
        if adv is None and adv_next in ready:
            adv = advance_chain(adv_next, *ready.pop(adv_next))
        if adv is not None:
            try:
                next(adv)
            except StopIteration:
                adv, adv_next = None, adv_next + 1
        rnd += 1

    for p in range(grp):
        h_ref[p] = hs[p]


def rwkv7_mix(proj, pp, w2p, a2p, g2p, layer, *, tt=512, grp=RWKV_GROUP):
    t = proj.shape[0]
    c2 = 2 * CHUNK
    gw = grp * LANES
    nb = gw // RWKV_BLOCK

    def col_blocks(col, width, count, per_group):
        return [pl.BlockSpec((tt, width), lambda p, i, o=col // width + n: (i, o + per_group * p))
                for n in range(count)]

    in_specs = (col_blocks(COL_R, RWKV_BLOCK, nb, nb) + col_blocks(COL_K, RWKV_BLOCK, nb, nb)
                + col_blocks(COL_V, RWKV_BLOCK, nb, nb) + col_blocks(COL_LORA, LANES, 4, 0))
    in_specs += [
        pl.BlockSpec((None, SUBLANES, gw), lambda p, i: (layer, 0, p)),
        pl.BlockSpec((None, LANES, gw), lambda p, i: (layer, 0, p)),
        pl.BlockSpec((None, 2 * LANES, gw), lambda p, i: (layer, 0, p)),
        pl.BlockSpec((None, 3 * LANES, gw), lambda p, i: (layer, 0, p)),
    ]
    return pl.pallas_call(
        functools.partial(_rwkv_kernel, tt=tt, grp=grp),
        out_shape=jax.ShapeDtypeStruct((t, RWKV_WIDTH), BF16),
        grid=(RWKV_PAIRS // grp, t // tt),
        in_specs=in_specs,
        out_specs=pl.BlockSpec((tt, gw), lambda p, i: (i, p)),
        scratch_shapes=[pltpu.VMEM((grp, c2, LANES), F32)],
        compiler_params=_cparams(("arbitrary", "arbitrary")),
        name="rwkv7_mix",
    )(*([proj] * (3 * nb + 4)), pp, w2p, a2p, g2p)


def _alibi_slopes():
    idx = np.arange(1, ALIBI_HEADS + 1, dtype=np.float32)
    m = np.exp2(-8.0 * idx / ALIBI_HEADS).astype(np.float32)
    diff_idx = np.arange(2, ALIBI_HEADS, 3)
    swa_idx = np.setdiff1d(np.arange(ALIBI_HEADS), diff_idx)
    return m[diff_idx], m[swa_idx]


def _pad_rows(w, before, total):
    return jnp.pad(w, ((0, 0), (before, total - before - w.shape[1]), (0, 0)))


def kernel(x, attn_norm_g, w_in, diff_lambda, diff_subln_g, swa_sinks, rwkv_mu, rwkv_w0, rwkv_w2,
           rwkv_a0, rwkv_a2, rwkv_g2, rwkv_k_k, rwkv_k_a, rwkv_r_k, rwkv_ln_w, rwkv_ln_b, w_out,
           ffn_norm_g, w_gate_up, w_down, final_norm_g):
    b, t, d = x.shape
    depth = w_in.shape[0]
    diff_slopes, swa_slopes = _alibi_slopes()
    diff_slopes = jnp.asarray(diff_slopes)

    w_down_b = w_down.astype(BF16)
    mu_cols = jnp.pad(rwkv_mu, ((0, 0), (IN_COLS - rwkv_mu.shape[1], 0))).reshape(depth, 1, IN_COLS)
    w2_b = _pad_rows(rwkv_w2, 0, LANES).astype(BF16)
    a2_b = _pad_rows(rwkv_a2, DECAY_LORA, 2 * LANES).astype(BF16)
    g2_b = _pad_rows(rwkv_g2, DECAY_LORA + ICLR_LORA - LANES, 3 * LANES).astype(BF16)
    pp = jnp.stack([rwkv_w0, rwkv_a0, rwkv_k_k, rwkv_k_a, rwkv_r_k.reshape(depth, RWKV_WIDTH),
                    rwkv_ln_w, rwkv_ln_b, jnp.zeros_like(rwkv_w0)], axis=1)
    attn_g = attn_norm_g.reshape(depth, 1, d)
    ffn_g = ffn_norm_g.reshape(depth, 1, d)

    outs = []
    for bi in range(b):
        xb = x[bi]
        for l in range(depth):
            lambda_init = 0.8 - 0.6 * math.exp(-0.3 * l)
            proj = norm_matmul(xb, attn_g, w_in, mu_cols, l)
            ya = diff_attention(proj, diff_lambda[l], diff_subln_g[l], diff_slopes,
                                lambda_init=lambda_init)
            yb = sliding_window_attention(proj, swa_sinks[l], swa_slopes)
            yc = rwkv7_mix(proj, pp, w2_b, a2_b, g2_b, l)
            xb = matmul_residual([ya, yb, yc], w_out, l, xb, tn=1024, name="out_proj")
            hid = ffn_up(xb, ffn_g, w_gate_up, l)
            xb = matmul_residual([hid], w_down_b, l, xb, name="ffn_down")
        outs.append(rmsnorm(xb, final_norm_g))
    return jnp.stack(outs, axis=0)
```

```python
import functools
import math

import numpy as np
import jax
import jax.numpy as jnp
from jax import lax
from jax.experimental import pallas as pl
from jax.experimental.pallas import tpu as pltpu

F32 = jnp.float32
BF16 = jnp.bfloat16

D_MODEL = 2048
LANES = 128
SUBLANES = 8
DIFF_HEADS = 4
DIFF_QK_DIM = 64
SWA_Q_HEADS = 8
SWA_HEAD_DIM = 64
SWA_WINDOW = 128
RWKV_HEADS = 16
RWKV_HEAD_DIM = 64
RWKV_WIDTH = RWKV_HEADS * RWKV_HEAD_DIM
RWKV_PAIRS = RWKV_WIDTH // LANES
DECAY_LORA = 96
ICLR_LORA = 96
GATE_LORA = 256
FFN_HIDDEN = 5632
ALIBI_HEADS = 12
NORM_EPS = 1e-5
RWKV_GN_EPS = 64e-5
NEG_INF = -1e30
CHUNK = 64
RWKV_GROUP = 4
CHUNK_STAGGER = 3

IN_COLS = 5824
COL_QA, COL_KA, COL_VA = 0, 512, 1024
COL_QB, COL_KB, COL_VB = 1536, 2048, 2176
COL_R, COL_K, COL_V = 2304, 3328, 4352
COL_LORA = 5376
PROJ_COLS = 6144
LORA_COLS = 512
RWKV_BLOCK = 2 * LANES
NORM_CHUNKS = 4

VMEM_LIMIT = 56 * 1024 * 1024


def _cparams(sem):
    return pltpu.CompilerParams(dimension_semantics=sem, vmem_limit_bytes=VMEM_LIMIT)


def _dot(a, b):
    return jnp.dot(a, b, preferred_element_type=F32)


def _dot_nt(a, b):
    return lax.dot_general(a, b, (((1,), (1,)), ((), ())), preferred_element_type=F32)


def _norm_rows(x, g):
    ms = jnp.mean(x * x, axis=-1, keepdims=True)
    return (x * lax.rsqrt(ms + NORM_EPS) * g).astype(BF16)


def _norm_then_dots(x_ref, g_ref, h_ref, weights):
    tm = x_ref.shape[0]
    cm = tm // NORM_CHUNKS
    g = g_ref[...]
    parts = []
    for c in range(NORM_CHUNKS):
        rows = slice(c * cm, (c + 1) * cm)
        h = _norm_rows(x_ref[rows, :], g)
        h_ref[rows, :] = h
        parts.append([_dot(h, w) for w in weights])
    return [jnp.concatenate([p[k] for p in parts], axis=0) for k in range(len(weights))]


def _norm_mm_kernel(x_ref, g_ref, w_ref, mu_ref, o_ref, h_ref, last_ref):
    i = pl.program_id(0)
    j = pl.program_id(1)
    w = w_ref[...]

    def finish(acc):
        rowi = lax.broadcasted_iota(jnp.int32, acc.shape, 0)
        first = jnp.where(i > 0, last_ref[j, SUBLANES - 1:SUBLANES, :], 0.0)
        prev = jnp.where(rowi == 0, first, pltpu.roll(acc, 1, axis=0))
        last_ref[j] = acc[acc.shape[0] - SUBLANES:, :]
        o_ref[...] = acc + (prev - acc) * mu_ref[...]

    @pl.when(j == 0)
    def _():
        finish(_norm_then_dots(x_ref, g_ref, h_ref, [w])[0])

    @pl.when(j > 0)
    def _():
        finish(_dot(h_ref[...], w))


def norm_matmul(x, g, w, mu, layer, *, tm=1024, tn=1024):
    t, d = x.shape
    n = w.shape[2]
    nj = pl.cdiv(n, tn)
    return pl.pallas_call(
        _norm_mm_kernel,
        out_shape=jax.ShapeDtypeStruct((t, n), F32),
        grid=(t // tm, nj),
        in_specs=[
            pl.BlockSpec((tm, d), lambda i, j: (i, 0)),
            pl.BlockSpec((None, 1, d), lambda i, j: (layer, 0, 0)),
            pl.BlockSpec((None, d, tn), lambda i, j: (layer, 0, j)),
            pl.BlockSpec((None, 1, tn), lambda i, j: (layer, 0, j)),
        ],
        out_specs=pl.BlockSpec((tm, tn), lambda i, j: (i, j)),
        scratch_shapes=[pltpu.VMEM((tm, d), BF16), pltpu.VMEM((nj, SUBLANES, tn), F32)],
        compiler_params=_cparams(("arbitrary", "arbitrary")),
        name="norm_in_proj",
    )(x, g, w, mu)


def _ffn_up_kernel(x_ref, g_ref, wg_ref, wu_ref, o_ref, h_ref):
    j = pl.program_id(1)
    wg = wg_ref[...].astype(BF16)
    wu = wu_ref[...].astype(BF16)

    def finish(gate, up):
        o_ref[...] = (gate / (1.0 + jnp.exp(-gate)) * up).astype(BF16)

    @pl.when(j == 0)
    def _():
        finish(*_norm_then_dots(x_ref, g_ref, h_ref, [wg, wu]))

    @pl.when(j > 0)
    def _():
        h = h_ref[...]
        finish(_dot(h, wg), _dot(h, wu))


def ffn_up(x, g, w_gate_up, layer, *, tm=1024, tn=512):
    t, d = x.shape
    hid = w_gate_up.shape[2] // 2
    nj = hid // tn
    return pl.pallas_call(
        _ffn_up_kernel,
        out_shape=jax.ShapeDtypeStruct((t, hid), BF16),
        grid=(t // tm, nj),
        in_specs=[
            pl.BlockSpec((tm, d), lambda i, j: (i, 0)),
            pl.BlockSpec((None, 1, d), lambda i, j: (layer, 0, 0)),
            pl.BlockSpec((None, d, tn), lambda i, j: (layer, 0, j)),
            pl.BlockSpec((None, d, tn), lambda i, j: (layer, 0, j + nj)),
        ],
        out_specs=pl.BlockSpec((tm, tn), lambda i, j: (i, j)),
        scratch_shapes=[pltpu.VMEM((tm, d), BF16)],
        compiler_params=_cparams(("parallel", "arbitrary")),
        name="ffn_up",
    )(x, g, w_gate_up, w_gate_up)


def _mm_res_kernel(*refs, splits):
    n = len(splits)
    a_refs, (w_ref, x_ref, o_ref) = refs[:n], refs[n:]
    acc = x_ref[...]
    off = 0
    for a_ref, k in zip(a_refs, splits):
        acc = acc + _dot(a_ref[...], w_ref[off:off + k, :].astype(BF16))
        off += k
    o_ref[...] = acc


def matmul_residual(a_list, w, layer, x, *, tm=1024, tn=512, name):
    t, n = x.shape
    splits = tuple(a.shape[1] for a in a_list)
    k = sum(splits)
    in_specs = [pl.BlockSpec((tm, s), lambda i, j: (i, 0)) for s in splits]
    in_specs += [
        pl.BlockSpec((None, k, tn), lambda i, j: (layer, 0, j)),
        pl.BlockSpec((tm, tn), lambda i, j: (i, j)),
    ]
    return pl.pallas_call(
        functools.partial(_mm_res_kernel, splits=splits),
        out_shape=jax.ShapeDtypeStruct((t, n), F32),
        grid=(t // tm, n // tn),
        in_specs=in_specs,
        out_specs=pl.BlockSpec((tm, tn), lambda i, j: (i, j)),
        compiler_params=_cparams(("parallel", "arbitrary")),
        name=name,
    )(*a_list, w, x)


def _rmsnorm_kernel(x_ref, g_ref, o_ref):
    x = x_ref[...]
    ms = jnp.mean(x * x, axis=-1, keepdims=True)
    o_ref[...] = x * lax.rsqrt(ms + NORM_EPS) * g_ref[...]


def rmsnorm(x, g, *, tm=512):
    t, d = x.shape
    return pl.pallas_call(
        _rmsnorm_kernel,
        out_shape=jax.ShapeDtypeStruct((t, d), F32),
        grid=(t // tm,),
        in_specs=[pl.BlockSpec((tm, d), lambda i: (i, 0)), pl.BlockSpec((1, d), lambda i: (0, 0))],
        out_specs=pl.BlockSpec((tm, d), lambda i: (i, 0)),
        compiler_params=_cparams(("parallel",)),
        name="final_rmsnorm",
    )(x, g.reshape(1, d))


LOG2E = 1.4426950408889634
POS_SPLIT = 128
ONES_ROWS = 16


def _pos_lanes(lane, base, cols):
    out = jnp.zeros(lane.shape, F32)
    for idx, col in enumerate(cols):
        out = jnp.where(lane == base + idx, col, out)
    return out


def _diff_attn_kernel(slope_ref, q_ref, k_ref, v_ref, lam_ref, g_ref, o_ref,
                      kb_ref, vt_ref, qs_ref, sa_ref, sb_ref, m_ref, acc_ref, *, tq, tk, nk,
                      lambda_init):
    h = pl.program_id(0)
    i = pl.program_id(1)
    kpq = tq // tk
    beta = jnp.full((1, LANES), slope_ref[h] * LOG2E, F32)
    b1 = beta.astype(BF16).astype(F32)
    b2 = (beta - b1).astype(BF16).astype(F32)
    b3 = (beta - b1 - b2).astype(BF16).astype(F32)
    own = [lambda lane, c=c: (lane < DIFF_QK_DIM) == (c == 0) for c in range(2)]
    pos_base = [DIFF_QK_DIM, 0]

    @pl.when(i == 0)
    def _():
        lane = lax.broadcasted_iota(jnp.int32, (tk, LANES), 1)
        krel = lax.broadcasted_iota(jnp.int32, (tk, LANES), 0)
        khi = ((krel // POS_SPLIT) * POS_SPLIT).astype(F32)
        klo = (krel % POS_SPLIT).astype(F32)
        kcols = []
        for b in (b1, b2, b3):
            kcols += [-b * POS_SPLIT, -b, khi, klo]
        posk = [_pos_lanes(lane, pos_base[c], kcols) for c in range(2)]
        ones = jnp.where(lax.broadcasted_iota(jnp.int32, (ONES_ROWS, tk), 0) == 0, 1.0, 0.0)

        def stage(j, carry):
            rows = pl.ds(pl.multiple_of(j * tk, tk), tk)
            k = k_ref[rows, :]
            for c in range(2):
                kb_ref[c, rows, :] = jnp.where(own[c](lane), k, posk[c]).astype(BF16)
            vt_ref[j] = jnp.concatenate([v_ref[rows, :].T, ones], axis=0).astype(BF16)
            return carry

        lax.fori_loop(0, nk, stage, 0)

    q = q_ref[...] * (DIFF_QK_DIM ** -0.5 * LOG2E)
    lane = lax.broadcasted_iota(jnp.int32, (tq, LANES), 1)
    qrel = lax.broadcasted_iota(jnp.int32, (tq, LANES), 0)
    qhi = (qrel // POS_SPLIT).astype(F32)
    qlo = (qrel % POS_SPLIT).astype(F32)
    qcols = []
    for b in (b1, b2, b3):
        qcols += [qhi, qlo, b, b]
    for c in range(2):
        qs_ref[c] = jnp.where(own[c](lane), q, _pos_lanes(lane, pos_base[c], qcols)).astype(BF16)
    m_ref[...] = jnp.full(m_ref.shape, NEG_INF, F32)
    acc_ref[...] = jnp.zeros(acc_ref.shape, F32)

    def scores(j, dst_ref, qlo):
        rows = pl.ds(pl.multiple_of(j * tk, tk), tk)

        def chain(c):
            s = _dot_nt(kb_ref[c, rows, :], qs_ref[c, qlo:, :])
            yield
            return s

        return [chain(c) for c in range(2)], dst_ref

    def consume(src_ref, j, qlo, masked):
        vt = vt_ref[j]
        c0 = (slope_ref[h] * LOG2E) * (j * tk - i * tq).astype(F32)

        def chain(c):
            s = src_ref[c, :, qlo:]
            if masked:
                key = lax.broadcasted_iota(jnp.int32, s.shape, 0)
                qry = lax.broadcasted_iota(jnp.int32, s.shape, 1)
                s = jnp.where(qry >= key, s, NEG_INF)
            m_prev = m_ref[c, :, qlo:]
            m_new = jnp.maximum(m_prev, jnp.max(s, axis=0, keepdims=True) + c0)
            yield
            p = jnp.exp2(s - (m_new - c0)).astype(BF16)
            alpha = jnp.exp2(m_prev - m_new)
            yield
            return alpha * acc_ref[c, :, qlo:] + _dot(vt, p), m_new

        return [chain(c) for c in range(2)]

    def step(score_args, consume_args):
        chains = consume(*consume_args)
        dst, dst_qlo = None, 0
        if score_args is not None:
            score_chains, dst = scores(*score_args)
            dst_qlo = score_args[2]
            chains = chains + score_chains
        results = _interleave(chains)
        qlo = consume_args[2]
        for c in range(2):
            acc_new, m_new = results[c]
            acc_ref[c, :, qlo:] = acc_new
            m_ref[c, :, qlo:] = m_new
            if dst is not None:
                dst[c, :, dst_qlo:] = results[2 + c]

    for c, s in enumerate(_interleave(scores(0, sa_ref, 0)[0])):
        sa_ref[c] = s

    def body(jj, carry):
        j = jj * kpq
        step((j + 1, sb_ref, 0), (sa_ref, j, 0, False))
        step((j + 2, sa_ref, 0), (sb_ref, j + 1, 0, False))
        return carry

    lax.fori_loop(0, i, body, 0)
    step((i * kpq + 1, sb_ref, tk), (sa_ref, i * kpq, 0, True))
    step(None, (sb_ref, i * kpq + 1, tk, True))

    lamv = lam_ref[...]
    s01 = jnp.sum(lamv[0:1] * lamv[1:2], axis=-1, keepdims=True)
    s23 = jnp.sum(lamv[2:3] * lamv[3:4], axis=-1, keepdims=True)
    lam = jnp.exp(s01) - jnp.exp(s23) + lambda_init
    a0, a1 = acc_ref[0], acc_ref[1]
    ot = a0[:LANES] / a0[LANES:LANES + 1] - lam * (a1[:LANES] / a1[LANES:LANES + 1])
    o = ot.T
    o = o * lax.rsqrt(jnp.mean(o * o, axis=-1, keepdims=True) + NORM_EPS)
    o_ref[...] = (o * g_ref[...] * (1.0 - lambda_init)).astype(BF16)


def diff_attention(proj, lamv, subln_g, slopes, *, lambda_init, tq=1024, tk=512):
    t = proj.shape[0]
    nk = t // tk
    assert tq == 2 * tk and t % tq == 0
    cb = lambda col: col // LANES
    return pl.pallas_call(
        functools.partial(_diff_attn_kernel, tq=tq, tk=tk, nk=nk, lambda_init=lambda_init),
        out_shape=jax.ShapeDtypeStruct((t, DIFF_HEADS * LANES), BF16),
        grid=(DIFF_HEADS, t // tq),
        in_specs=[
            pl.BlockSpec(memory_space=pltpu.SMEM),
            pl.BlockSpec((tq, LANES), lambda h, i: (i, cb(COL_QA) + h)),
            pl.BlockSpec((t, LANES), lambda h, i: (0, cb(COL_KA) + h)),
            pl.BlockSpec((t, LANES), lambda h, i: (0, cb(COL_VA) + h)),
            pl.BlockSpec((4, DIFF_QK_DIM), lambda h, i: (0, 0)),
            pl.BlockSpec((1, LANES), lambda h, i: (0, 0)),
        ],
        out_specs=pl.BlockSpec((tq, LANES), lambda h, i: (i, h)),
        scratch_shapes=[
            pltpu.VMEM((2, t, LANES), BF16),
            pltpu.VMEM((nk, LANES + ONES_ROWS, tk), BF16),
            pltpu.VMEM((2, tq, LANES), BF16),
            pltpu.VMEM((2, tk, tq), F32),
            pltpu.VMEM((2, tk, tq), F32),
            pltpu.VMEM((2, 1, tq), F32),
            pltpu.VMEM((2, LANES + ONES_ROWS, tq), F32),
        ],
        compiler_params=_cparams(("arbitrary", "arbitrary")),
        name="diff_attention",
    )(slopes, proj, proj, proj, lamv, subln_g.reshape(1, LANES))


def _swa_kernel(sink_ref, qa_ref, qb_ref, k_ref, v_ref, kh_ref, vh_ref, o_ref, *, tq, slopes):
    i = pl.program_id(0)
    w = SWA_WINDOW
    low = lax.broadcasted_iota(jnp.int32, (w, LANES), 1) < SWA_HEAD_DIM
    low2 = lax.broadcasted_iota(jnp.int32, (2 * w, LANES), 1) < SWA_HEAD_DIM
    row = lax.broadcasted_iota(jnp.int32, (w, 2 * w), 0)
    col = lax.broadcasted_iota(jnp.int32, (w, 2 * w), 1)
    dist = row + w - col
    valid = jnp.logical_and(dist >= 0, dist < w)
    valid0 = jnp.logical_and(valid, jnp.logical_or(col >= w, i > 0))
    distf = dist.astype(F32)
    q_refs = (qa_ref, qb_ref)

    def one_head(qh, kd, vd, ok, hq):
        s = _dot_nt(qh, kd)
        yield
        s = jnp.where(ok, s - slopes[hq] * distf, NEG_INF)
        sink = sink_ref[hq]
        m = jnp.maximum(jnp.max(s, axis=-1, keepdims=True), sink)
        yield
        e = jnp.exp(s - m)
        den = jnp.sum(e, axis=-1, keepdims=True) + jnp.exp(sink - m)
        yield
        return _dot((e / den).astype(BF16), vd)

    chains, where_to = [], []
    for r in range(tq // w):
        rs = slice(r * w, (r + 1) * w)
        ok = valid0 if r == 0 else valid
        if r == 0:
            kband = jnp.concatenate([kh_ref[...], k_ref[0:w, :]], axis=0)
            vband = jnp.concatenate([vh_ref[...], v_ref[0:w, :]], axis=0)
        else:
            kband = k_ref[(r - 1) * w:(r + 1) * w, :]
            vband = v_ref[(r - 1) * w:(r + 1) * w, :]
        kroll = pltpu.roll(kband, SWA_HEAD_DIM, axis=1)
        vroll = pltpu.roll(vband, SWA_HEAD_DIM, axis=1)
        for kv in range(2):
            kd = (jnp.where(low2, kband, kroll) if kv == 0 else jnp.where(low2, kroll, kband))
            vd = (jnp.where(low2, vband, vroll) if kv == 0 else jnp.where(low2, vroll, vband))
            kd = kd.astype(BF16)
            vd = vd.astype(BF16)
            for j in range(2):
                jj = kv * 2 + j
                q = q_refs[kv][rs, j * LANES:(j + 1) * LANES] * (SWA_HEAD_DIM ** -0.5)
                for half in range(2):
                    qh = jnp.where(low, q, 0.0) if half == 0 else jnp.where(low, 0.0, q)
                    chains.append(one_head(qh.astype(BF16), kd, vd, ok, 2 * jj + half))
                where_to.append((rs, jj))
    outs = _interleave(chains)
    for n, (rs, jj) in enumerate(where_to):
        o_ref[rs, jj * LANES:(jj + 1) * LANES] = jnp.where(low, outs[2 * n], outs[2 * n + 1]).astype(BF16)


def sliding_window_attention(proj, sinks, slopes, *, tq=512):
    t = proj.shape[0]
    w = SWA_WINDOW
    qw = SWA_Q_HEADS * SWA_HEAD_DIM
    rpb = tq // w
    qblk = COL_QB // (2 * LANES)
    kblk, vblk = COL_KB // LANES, COL_VB // LANES
    return pl.pallas_call(
        functools.partial(_swa_kernel, tq=tq, slopes=tuple(float(s) for s in slopes)),
        out_shape=jax.ShapeDtypeStruct((t, qw), BF16),
        grid=(t // tq,),
        in_specs=[
            pl.BlockSpec(memory_space=pltpu.SMEM),
            pl.BlockSpec((tq, 2 * LANES), lambda i: (i, qblk)),
            pl.BlockSpec((tq, 2 * LANES), lambda i: (i, qblk + 1)),
            pl.BlockSpec((tq, LANES), lambda i: (i, kblk)),
            pl.BlockSpec((tq, LANES), lambda i: (i, vblk)),
            pl.BlockSpec((w, LANES), lambda i: (jnp.maximum(i * rpb - 1, 0), kblk)),
            pl.BlockSpec((w, LANES), lambda i: (jnp.maximum(i * rpb - 1, 0), vblk)),
        ],
        out_specs=pl.BlockSpec((tq, qw), lambda i: (i, 0)),
        compiler_params=_cparams(("parallel",)),
        name="swa_attention",
    )(sinks, proj, proj, proj, proj, proj, proj)


def _interleave(gens):
    gens = list(gens)
    results = [None] * len(gens)
    live = list(range(len(gens)))
    while live:
        still = []
        for idx in live:
            try:
                next(gens[idx])
                still.append(idx)
            except StopIteration as done:
                results[idx] = done.value
        live = still
    return results


def _seg_sum(x, low):
    s0 = jnp.sum(jnp.where(low, x, 0.0), axis=-1, keepdims=True)
    s1 = jnp.sum(jnp.where(low, 0.0, x), axis=-1, keepdims=True)
    return jnp.where(low, s0, s1)


def _stack_heads(z, low):
    zero = jnp.zeros_like(z)
    return jnp.concatenate([jnp.where(low, z, zero), jnp.where(low, zero, z)], axis=0)


def _round_robin(subs, results):
    live = list(range(len(subs)))
    while live:
        still = []
        for idx in live:
            try:
                next(subs[idx])
                still.append(idx)
            except StopIteration as done:
                results[idx] = done.value
        live = still
        yield


def _rwkv_kernel(*refs, tt, grp):
    nb = grp * LANES // RWKV_BLOCK
    r_refs, k_refs, v_refs = refs[0:nb], refs[nb:2 * nb], refs[2 * nb:3 * nb]
    lo_refs = refs[3 * nb:3 * nb + 4]
    _rwkv_body(r_refs, k_refs, v_refs, lo_refs, *refs[3 * nb + 4:], tt=tt, grp=grp)


def _rwkv_body(r_refs, k_refs, v_refs, lo_refs, pp_ref, w2_ref, a2_ref, g2_ref, y_ref, h_ref, *,
               tt, grp):
    i = pl.program_id(1)
    nch = tt // CHUNK
    c2 = 2 * CHUNK
    gw = grp * LANES

    @pl.when(i == 0)
    def _():
        h_ref[...] = jnp.zeros(h_ref.shape, F32)

    pp = pp_ref[...]
    w0, a0, k_k, k_a, r_k, ln_w, ln_b = (pp[0:1], pp[1:2], pp[2:3], pp[3:4], pp[4:5], pp[5:6],
                                         pp[6:7])
    low_c = lax.broadcasted_iota(jnp.int32, (CHUNK, LANES), 1) < RWKV_HEAD_DIM
    rr = lax.broadcasted_iota(jnp.int32, (CHUNK, LANES), 0)
    cc = lax.broadcasted_iota(jnp.int32, (CHUNK, LANES), 1) & (RWKV_HEAD_DIM - 1)
    lt = rr > cc
    le = rr >= cc
    eye = jnp.where(rr == cc, 1.0, 0.0)
    bd = (lax.broadcasted_iota(jnp.int32, (c2, LANES), 0) < CHUNK) == (
        lax.broadcasted_iota(jnp.int32, (c2, LANES), 1) < RWKV_HEAD_DIM)
    rowc = lax.broadcasted_iota(jnp.int32, (CHUNK, gw), 0)

    def seg_sum(x):
        return jnp.concatenate(
            [_seg_sum(x[:, p * LANES:(p + 1) * LANES], low_c) for p in range(grp)], axis=1)

    def prepare_chain(at, rt, bt, kt, vs, bh, khat, etot):
        sb = _stack_heads(bt, low_c)
        sk = _stack_heads(kt, low_c)
        sv = _stack_heads(vs, low_c)
        a4 = _dot_nt(jnp.concatenate([at, rt], axis=0), jnp.concatenate([sb, sk], axis=0))
        kv = jnp.where(bd, _dot(khat.T.astype(BF16), vs), 0.0)
        bht = bh.T.astype(BF16)
        dx = jnp.broadcast_to(etot, (c2, LANES)).T
        yield
        n = jnp.where(lt, a4[:CHUNK, :LANES], 0.0)
        aak = jnp.where(lt, a4[:CHUNK, LANES:], 0.0)
        arb = jnp.where(le, a4[CHUNK:, :LANES], 0.0)
        ark = jnp.where(le, a4[CHUNK:, LANES:], 0.0)
        av = _dot(jnp.concatenate([aak, ark], axis=0).astype(BF16), sv)
        tinv = eye + n
        pw = n
        for _ in range(int(math.log2(CHUNK)) - 1):
            pb = pw.astype(BF16)
            pw = _dot(pb, _stack_heads(pb, low_c))
            yield
            tinv = tinv + _dot(pw.astype(BF16), _stack_heads(tinv.astype(BF16), low_c))
            yield
        rhs = jnp.concatenate([_stack_heads(at, low_c),
                               _stack_heads(av[:CHUNK].astype(BF16), low_c)], axis=1)
        wu = _dot(tinv.astype(BF16), rhs)
        yield
        return (jnp.concatenate([wu[:, :LANES].astype(BF16), rt], axis=0), wu[:, LANES:],
                av[CHUNK:], arb.astype(BF16), bht, kv, dx)

    def chunk_chain(c):
        rows = slice(c * CHUNK, (c + 1) * CHUNK)
        lo0, lo1, lo2, lo3 = (lo_refs[n][rows, :] for n in range(4))
        z = -(w0 + _dot(jnp.tanh(lo0).astype(BF16), w2_ref[...]))
        a_pre = a0 + _dot(jnp.concatenate([lo0, lo1], axis=1).astype(BF16), a2_ref[...])
        gate_in = jnp.concatenate([lo1, lo2, lo3], axis=1)
        g = _dot((1.0 / (1.0 + jnp.exp(-gate_in))).astype(BF16), g2_ref[...])
        yield
        r, k, v = (jnp.concatenate([ref[rows, :] for ref in refs_], axis=1)
                   for refs_ in (r_refs, k_refs, v_refs))
        softplus = jnp.maximum(z, 0.0) + jnp.log(1.0 + jnp.exp(-jnp.abs(z)))
        lw = -jnp.exp(-softplus - 0.5)
        a = 1.0 / (1.0 + jnp.exp(-a_pre))
        kx = k * k_k
        kk = kx * lax.rsqrt(jnp.maximum(seg_sum(kx * kx), 1e-24))
        k2 = k * (1.0 + (a - 1.0) * k_a)
        bvec = kk * a
        cum = lw
        for s in (1, 2, 4, 8, 16, 32):
            cum = cum + jnp.where(rowc >= s, pltpu.roll(cum, s, axis=0), 0.0)
        tot = cum[CHUNK - 1:CHUNK, :]
        einv = jnp.exp(-cum)
        erem = jnp.exp(tot - cum)
        at = (-kk * jnp.exp(cum - lw)).astype(BF16)
        rt = (r * jnp.exp(cum)).astype(BF16)
        bt = (bvec * einv).astype(BF16)
        kt = (k2 * einv).astype(BF16)
        bh = bvec * erem
        khat = k2 * erem
        vs = v.astype(BF16)
        etot = jnp.exp(tot)
        bonus = seg_sum(r * k2 * r_k) * v
        yield
        pairs = [None] * grp
        subs = []
        for p in range(grp):
            ln = slice(p * LANES, (p + 1) * LANES)
            subs.append(prepare_chain(at[:, ln], rt[:, ln], bt[:, ln], kt[:, ln], vs[:, ln],
                                      bh[:, ln], khat[:, ln], etot[:, ln]))
        yield from _round_robin(subs, pairs)
        return pairs, g, bonus

    hs = [h_ref[p] for p in range(grp)]

    def advance_chain(c, pairs, g, bonus):
        rows = slice(c * CHUNK, (c + 1) * CHUNK)
        whs = [_dot(pairs[p][0], hs[p].astype(BF16)) for p in range(grp)]
        yield
        ys = []
        for p in range(grp):
            _, u0, y0, arb, bht, kv, dx = pairs[p]
            u = (whs[p][:CHUNK] + u0).astype(BF16)
            ys.append(whs[p][CHUNK:] + _dot(arb, _stack_heads(u, low_c)) + y0)
            hs[p] = dx * hs[p] + jnp.where(bd, _dot(bht, u), 0.0) + kv
        yield
        y = jnp.concatenate(ys, axis=1)
        mean = seg_sum(y) * (1.0 / RWKV_HEAD_DIM)
        yc = y - mean
        var = seg_sum(yc * yc) * (1.0 / RWKV_HEAD_DIM)
        yn = yc * lax.rsqrt(var + RWKV_GN_EPS) * ln_w + ln_b
        y_ref[rows, :] = ((yn + bonus) * g).astype(BF16)

    chains = [chunk_chain(c) for c in range(nch)]
    ready = {}
    active, started, rnd = [], 0, 0
    adv, adv_next = None, 0
    while adv_next < nch:
        if started < nch and rnd % CHUNK_STAGGER == 0:
            active.append(started)
            started += 1
        for c in list(active):
            try:
                next(chains[c])
            except StopIteration as done:
                ready[c] = done.value
                active.remove(c)
        if adv is None and adv_next in ready:
            adv = advance_chain(adv_next, *ready.pop(adv_next))
        if adv is not None:
            try:
                next(adv)
            except StopIteration:
                adv, adv_next = None, adv_next + 1
        rnd += 1

    for p in range(grp):
        h_ref[p] = hs[p]


def rwkv7_mix(proj, pp, w2p, a2p, g2p, layer, *, tt=512, grp=RWKV_GROUP):
    t = proj.shape[0]
    c2 = 2 * CHUNK
    gw = grp * LANES
    nb = gw // RWKV_BLOCK

    def col_blocks(col, width, count, per_group):
        return [pl.BlockSpec((tt, width), lambda p, i, o=col // width + n: (i, o + per_group * p))
                for n in range(count)]

    in_specs = (col_blocks(COL_R, RWKV_BLOCK, nb, nb) + col_blocks(COL_K, RWKV_BLOCK, nb, nb)
                + col_blocks(COL_V, RWKV_BLOCK, nb, nb) + col_blocks(COL_LORA, LANES, 4, 0))
    in_specs += [
        pl.BlockSpec((None, SUBLANES, gw), lambda p, i: (layer, 0, p)),
        pl.BlockSpec((None, LANES, gw), lambda p, i: (layer, 0, p)),
        pl.BlockSpec((None, 2 * LANES, gw), lambda p, i: (layer, 0, p)),
        pl.BlockSpec((None, 3 * LANES, gw), lambda p, i: (layer, 0, p)),
    ]
    return pl.pallas_call(
        functools.partial(_rwkv_kernel, tt=tt, grp=grp),
        out_shape=jax.ShapeDtypeStruct((t, RWKV_WIDTH), BF16),
        grid=(RWKV_PAIRS // grp, t // tt),
        in_specs=in_specs,
        out_specs=pl.BlockSpec((tt, gw), lambda p, i: (i, p)),
        scratch_shapes=[pltpu.VMEM((grp, c2, LANES), F32)],
        compiler_params=_cparams(("arbitrary", "arbitrary")),
        name="rwkv7_mix",
    )(*([proj] * (3 * nb + 4)), pp, w2p, a2p, g2p)


def _alibi_slopes():
    idx = np.arange(1, ALIBI_HEADS + 1, dtype=np.float32)
    m = np.exp2(-8.0 * idx / ALIBI_HEADS).astype(np.float32)
    diff_idx = np.arange(2, ALIBI_HEADS, 3)
    swa_idx = np.setdiff1d(np.arange(ALIBI_HEADS), diff_idx)
    return m[diff_idx], m[swa_idx]


def _pad_rows(w, before, total):
    return jnp.pad(w, ((0, 0), (before, total - before - w.shape[1]), (0, 0)))


def kernel(x, attn_norm_g, w_in, diff_lambda, diff_subln_g, swa_sinks, rwkv_mu, rwkv_w0, rwkv_w2,
           rwkv_a0, rwkv_a2, rwkv_g2, rwkv_k_k, rwkv_k_a, rwkv_r_k, rwkv_ln_w, rwkv_ln_b, w_out,
           ffn_norm_g, w_gate_up, w_down, final_norm_g):
    b, t, d = x.shape
    depth = w_in.shape[0]
    diff_slopes, swa_slopes = _alibi_slopes()
    diff_slopes = jnp.asarray(diff_slopes)

    w_in_b = jnp.pad(w_in.astype(BF16), ((0, 0), (0, 0), (0, PROJ_COLS - IN_COLS)))
    w_down_b = w_down.astype(BF16)
    mu_cols = jnp.pad(rwkv_mu, ((0, 0), (IN_COLS - rwkv_mu.shape[1], PROJ_COLS - IN_COLS))).reshape(
        depth, 1, PROJ_COLS)
    w2_b = _pad_rows(rwkv_w2, 0, LANES).astype(BF16)
    a2_b = _pad_rows(rwkv_a2, DECAY_LORA, 2 * LANES).astype(BF16)
    g2_b = _pad_rows(rwkv_g2, DECAY_LORA + ICLR_LORA - LANES, 3 * LANES).astype(BF16)
    pp = jnp.stack([rwkv_w0, rwkv_a0, rwkv_k_k, rwkv_k_a, rwkv_r_k.reshape(depth, RWKV_WIDTH),
                    rwkv_ln_w, rwkv_ln_b, jnp.zeros_like(rwkv_w0)], axis=1)
    attn_g = attn_norm_g.reshape(depth, 1, d)
    ffn_g = ffn_norm_g.reshape(depth, 1, d)

    outs = []
    for bi in range(b):
        xb = x[bi]
        for l in range(depth):
            lambda_init = 0.8 - 0.6 * math.exp(-0.3 * l)
            proj = norm_matmul(xb, attn_g, w_in_b, mu_cols, l)
            ya = diff_attention(proj, diff_lambda[l], diff_subln_g[l], diff_slopes,
                                lambda_init=lambda_init)
            yb = sliding_window_attention(proj, swa_sinks[l], swa_slopes)
            yc = rwkv7_mix(proj, pp, w2_b, a2_b, g2_b, l)
            xb = matmul_residual([ya, yb, yc], w_out, l, xb, tn=1024, name="out_proj")
            hid = ffn_up(xb, ffn_g, w_gate_up, l)
            xb = matmul_residual([hid], w_down_b, l, xb, name="ffn_down")
        outs.append(rmsnorm(xb, final_norm_g))
    return jnp.stack(outs, axis=0)
```

```python
import functools
import math

import numpy as np
import jax
import jax.numpy as jnp
from jax import lax
from jax.experimental import pallas as pl
from jax.experimental.pallas import tpu as pltpu

F32 = jnp.float32
BF16 = jnp.bfloat16

D_MODEL = 2048
LANES = 128
SUBLANES = 8
DIFF_HEADS = 4
DIFF_QK_DIM = 64
SWA_Q_HEADS = 8
SWA_HEAD_DIM = 64
SWA_WINDOW = 128
RWKV_HEADS = 16
RWKV_HEAD_DIM = 64
RWKV_WIDTH = RWKV_HEADS * RWKV_HEAD_DIM
RWKV_PAIRS = RWKV_WIDTH // LANES
DECAY_LORA = 96
ICLR_LORA = 96
GATE_LORA = 256
FFN_HIDDEN = 5632
ALIBI_HEADS = 12
NORM_EPS = 1e-5
RWKV_GN_EPS = 64e-5
NEG_INF = -1e30
CHUNK = 64
RWKV_GROUP = 8
CHUNK_STAGGER = 3

IN_COLS = 5824
COL_QA, COL_KA, COL_VA = 0, 512, 1024
COL_QB, COL_KB, COL_VB = 1536, 2048, 2176
COL_R, COL_K, COL_V = 2304, 3328, 4352
COL_LORA = 5376
PROJ_COLS = 6144
LORA_COLS = 512
RWKV_BLOCK = 2 * LANES
NORM_CHUNKS = 4

VMEM_LIMIT = 56 * 1024 * 1024


def _cparams(sem):
    return pltpu.CompilerParams(dimension_semantics=sem, vmem_limit_bytes=VMEM_LIMIT)


def _dot(a, b):
    return jnp.dot(a, b, preferred_element_type=F32)


def _dot_nt(a, b):
    return lax.dot_general(a, b, (((1,), (1,)), ((), ())), preferred_element_type=F32)


def _cast_pad_kernel(w_ref, o_ref):
    n = w_ref.shape[1]
    o_ref[:, :n] = w_ref[...].astype(BF16)
    o_ref[:, n:] = jnp.zeros((o_ref.shape[0], o_ref.shape[1] - n), BF16)


def cast_pad_cols(w, cols, *, tr=512):
    depth, rows, n = w.shape
    return pl.pallas_call(
        _cast_pad_kernel,
        out_shape=jax.ShapeDtypeStruct((depth, rows, cols), BF16),
        grid=(depth, rows // tr),
        in_specs=[pl.BlockSpec((None, tr, n), lambda l, i: (l, i, 0))],
        out_specs=pl.BlockSpec((None, tr, cols), lambda l, i: (l, i, 0)),
        compiler_params=_cparams(("parallel", "parallel")),
        name="cast_pad_w_in",
    )(w)


def _norm_rows(x, g):
    ms = jnp.mean(x * x, axis=-1, keepdims=True)
    return (x * lax.rsqrt(ms + NORM_EPS) * g).astype(BF16)


def _norm_then_dots(x_ref, g_ref, h_ref, weights):
    tm = x_ref.shape[0]
    cm = tm // NORM_CHUNKS
    g = g_ref[...]
    parts = []
    for c in range(NORM_CHUNKS):
        rows = slice(c * cm, (c + 1) * cm)
        h = _norm_rows(x_ref[rows, :], g)
        h_ref[rows, :] = h
        parts.append([_dot(h, w) for w in weights])
    return [jnp.concatenate([p[k] for p in parts], axis=0) for k in range(len(weights))]


def _norm_mm_kernel(x_ref, g_ref, w_ref, mu_ref, o_ref, h_ref, last_ref):
    i = pl.program_id(0)
    j = pl.program_id(1)
    w = w_ref[...]

    def finish(acc):
        rowi = lax.broadcasted_iota(jnp.int32, acc.shape, 0)
        first = jnp.where(i > 0, last_ref[j, SUBLANES - 1:SUBLANES, :], 0.0)
        prev = jnp.where(rowi == 0, first, pltpu.roll(acc, 1, axis=0))
        last_ref[j] = acc[acc.shape[0] - SUBLANES:, :]
        o_ref[...] = acc + (prev - acc) * mu_ref[...]

    @pl.when(j == 0)
    def _():
        finish(_norm_then_dots(x_ref, g_ref, h_ref, [w])[0])

    @pl.when(j > 0)
    def _():
        finish(_dot(h_ref[...], w))


def norm_matmul(x, g, w, mu, layer, *, tm=1024, tn=1024):
    t, d = x.shape
    n = w.shape[2]
    nj = pl.cdiv(n, tn)
    return pl.pallas_call(
        _norm_mm_kernel,
        out_shape=jax.ShapeDtypeStruct((t, n), F32),
        grid=(t // tm, nj),
        in_specs=[
            pl.BlockSpec((tm, d), lambda i, j: (i, 0)),
            pl.BlockSpec((None, 1, d), lambda i, j: (layer, 0, 0)),
            pl.BlockSpec((None, d, tn), lambda i, j: (layer, 0, j)),
            pl.BlockSpec((None, 1, tn), lambda i, j: (layer, 0, j)),
        ],
        out_specs=pl.BlockSpec((tm, tn), lambda i, j: (i, j)),
        scratch_shapes=[pltpu.VMEM((tm, d), BF16), pltpu.VMEM((nj, SUBLANES, tn), F32)],
        compiler_params=_cparams(("arbitrary", "arbitrary")),
        name="norm_in_proj",
    )(x, g, w, mu)


def _ffn_up_kernel(x_ref, g_ref, wg_ref, wu_ref, o_ref, h_ref):
    j = pl.program_id(1)
    wg = wg_ref[...].astype(BF16)
    wu = wu_ref[...].astype(BF16)

    def finish(gate, up):
        o_ref[...] = (gate / (1.0 + jnp.exp(-gate)) * up).astype(BF16)

    @pl.when(j == 0)
    def _():
        finish(*_norm_then_dots(x_ref, g_ref, h_ref, [wg, wu]))

    @pl.when(j > 0)
    def _():
        h = h_ref[...]
        finish(_dot(h, wg), _dot(h, wu))


def ffn_up(x, g, w_gate_up, layer, *, tm=1024, tn=512):
    t, d = x.shape
    hid = w_gate_up.shape[2] // 2
    nj = hid // tn
    return pl.pallas_call(
        _ffn_up_kernel,
        out_shape=jax.ShapeDtypeStruct((t, hid), BF16),
        grid=(t // tm, nj),
        in_specs=[
            pl.BlockSpec((tm, d), lambda i, j: (i, 0)),
            pl.BlockSpec((None, 1, d), lambda i, j: (layer, 0, 0)),
            pl.BlockSpec((None, d, tn), lambda i, j: (layer, 0, j)),
            pl.BlockSpec((None, d, tn), lambda i, j: (layer, 0, j + nj)),
        ],
        out_specs=pl.BlockSpec((tm, tn), lambda i, j: (i, j)),
        scratch_shapes=[pltpu.VMEM((tm, d), BF16)],
        compiler_params=_cparams(("parallel", "arbitrary")),
        name="ffn_up",
    )(x, g, w_gate_up, w_gate_up)


def _mm_res_kernel(*refs, splits):
    n = len(splits)
    a_refs, (w_ref, x_ref, o_ref) = refs[:n], refs[n:]
    acc = x_ref[...]
    off = 0
    for a_ref, k in zip(a_refs, splits):
        acc = acc + _dot(a_ref[...], w_ref[off:off + k, :].astype(BF16))
        off += k
    o_ref[...] = acc


def matmul_residual(a_list, w, layer, x, *, tm=1024, tn=512, name):
    t, n = x.shape
    splits = tuple(a.shape[1] for a in a_list)
    k = sum(splits)
    in_specs = [pl.BlockSpec((tm, s), lambda i, j: (i, 0)) for s in splits]
    in_specs += [
        pl.BlockSpec((None, k, tn), lambda i, j: (layer, 0, j)),
        pl.BlockSpec((tm, tn), lambda i, j: (i, j)),
    ]
    return pl.pallas_call(
        functools.partial(_mm_res_kernel, splits=splits),
        out_shape=jax.ShapeDtypeStruct((t, n), F32),
        grid=(t // tm, n // tn),
        in_specs=in_specs,
        out_specs=pl.BlockSpec((tm, tn), lambda i, j: (i, j)),
        compiler_params=_cparams(("parallel", "arbitrary")),
        name=name,
    )(*a_list, w, x)


def _rmsnorm_kernel(x_ref, g_ref, o_ref):
    x = x_ref[...]
    ms = jnp.mean(x * x, axis=-1, keepdims=True)
    o_ref[...] = x * lax.rsqrt(ms + NORM_EPS) * g_ref[...]


def rmsnorm(x, g, *, tm=512):
    t, d = x.shape
    return pl.pallas_call(
        _rmsnorm_kernel,
        out_shape=jax.ShapeDtypeStruct((t, d), F32),
        grid=(t // tm,),
        in_specs=[pl.BlockSpec((tm, d), lambda i: (i, 0)), pl.BlockSpec((1, d), lambda i: (0, 0))],
        out_specs=pl.BlockSpec((tm, d), lambda i: (i, 0)),
        compiler_params=_cparams(("parallel",)),
        name="final_rmsnorm",
    )(x, g.reshape(1, d))


LOG2E = 1.4426950408889634
POS_SPLIT = 128
ONES_ROWS = 16


def _pos_lanes(lane, base, cols):
    out = jnp.zeros(lane.shape, F32)
    for idx, col in enumerate(cols):
        out = jnp.where(lane == base + idx, col, out)
    return out


def _diff_attn_kernel(slope_ref, q_ref, k_ref, v_ref, lam_ref, g_ref, o_ref,
                      kb_ref, vt_ref, qs_ref, sa_ref, sb_ref, m_ref, acc_ref, *, tq, tk, nk,
                      lambda_init):
    h = pl.program_id(0)
    i = pl.program_id(1)
    kpq = tq // tk
    beta = jnp.full((1, LANES), slope_ref[h] * LOG2E, F32)
    b1 = beta.astype(BF16).astype(F32)
    b2 = (beta - b1).astype(BF16).astype(F32)
    b3 = (beta - b1 - b2).astype(BF16).astype(F32)
    own = [lambda lane, c=c: (lane < DIFF_QK_DIM) == (c == 0) for c in range(2)]
    pos_base = [DIFF_QK_DIM, 0]

    @pl.when(i == 0)
    def _():
        lane = lax.broadcasted_iota(jnp.int32, (tk, LANES), 1)
        krel = lax.broadcasted_iota(jnp.int32, (tk, LANES), 0)
        khi = ((krel // POS_SPLIT) * POS_SPLIT).astype(F32)
        klo = (krel % POS_SPLIT).astype(F32)
        kcols = []
        for b in (b1, b2, b3):
            kcols += [-b * POS_SPLIT, -b, khi, klo]
        posk = [_pos_lanes(lane, pos_base[c], kcols) for c in range(2)]
        ones = jnp.where(lax.broadcasted_iota(jnp.int32, (ONES_ROWS, tk), 0) == 0, 1.0, 0.0)

        def stage(j, carry):
            rows = pl.ds(pl.multiple_of(j * tk, tk), tk)
            k = k_ref[rows, :]
            for c in range(2):
                kb_ref[c, rows, :] = jnp.where(own[c](lane), k, posk[c]).astype(BF16)
            vt_ref[j] = jnp.concatenate([v_ref[rows, :].T, ones], axis=0).astype(BF16)
            return carry

        lax.fori_loop(0, nk, stage, 0)

    q = q_ref[...] * (DIFF_QK_DIM ** -0.5 * LOG2E)
    lane = lax.broadcasted_iota(jnp.int32, (tq, LANES), 1)
    qrel = lax.broadcasted_iota(jnp.int32, (tq, LANES), 0)
    qhi = (qrel // POS_SPLIT).astype(F32)
    qlo = (qrel % POS_SPLIT).astype(F32)
    qcols = []
    for b in (b1, b2, b3):
        qcols += [qhi, qlo, b, b]
    for c in range(2):
        qs_ref[c] = jnp.where(own[c](lane), q, _pos_lanes(lane, pos_base[c], qcols)).astype(BF16)
    m_ref[...] = jnp.full(m_ref.shape, NEG_INF, F32)
    acc_ref[...] = jnp.zeros(acc_ref.shape, F32)

    def scores(j, dst_ref, qlo):
        rows = pl.ds(pl.multiple_of(j * tk, tk), tk)

        def chain(c):
            s = _dot_nt(kb_ref[c, rows, :], qs_ref[c, qlo:, :])
            yield
            return s

        return [chain(c) for c in range(2)], dst_ref

    def consume(src_ref, j, qlo, masked):
        vt = vt_ref[j]
        c0 = (slope_ref[h] * LOG2E) * (j * tk - i * tq).astype(F32)

        def chain(c):
            s = src_ref[c, :, qlo:]
            if masked:
                key = lax.broadcasted_iota(jnp.int32, s.shape, 0)
                qry = lax.broadcasted_iota(jnp.int32, s.shape, 1)
                s = jnp.where(qry >= key, s, NEG_INF)
            m_prev = m_ref[c, :, qlo:]
            m_new = jnp.maximum(m_prev, jnp.max(s, axis=0, keepdims=True) + c0)
            yield
            p = jnp.exp2(s - (m_new - c0)).astype(BF16)
            alpha = jnp.exp2(m_prev - m_new)
            yield
            return alpha * acc_ref[c, :, qlo:] + _dot(vt, p), m_new

        return [chain(c) for c in range(2)]

    def step(score_args, consume_args):
        chains = consume(*consume_args)
        dst, dst_qlo = None, 0
        if score_args is not None:
            score_chains, dst = scores(*score_args)
            dst_qlo = score_args[2]
            chains = chains + score_chains
        results = _interleave(chains)
        qlo = consume_args[2]
        for c in range(2):
            acc_new, m_new = results[c]
            acc_ref[c, :, qlo:] = acc_new
            m_ref[c, :, qlo:] = m_new
            if dst is not None:
                dst[c, :, dst_qlo:] = results[2 + c]

    for c, s in enumerate(_interleave(scores(0, sa_ref, 0)[0])):
        sa_ref[c] = s

    def body(jj, carry):
        j = jj * kpq
        step((j + 1, sb_ref, 0), (sa_ref, j, 0, False))
        step((j + 2, sa_ref, 0), (sb_ref, j + 1, 0, False))
        return carry

    lax.fori_loop(0, i, body, 0)
    step((i * kpq + 1, sb_ref, tk), (sa_ref, i * kpq, 0, True))
    step(None, (sb_ref, i * kpq + 1, tk, True))

    lamv = lam_ref[...]
    s01 = jnp.sum(lamv[0:1] * lamv[1:2], axis=-1, keepdims=True)
    s23 = jnp.sum(lamv[2:3] * lamv[3:4], axis=-1, keepdims=True)
    lam = jnp.exp(s01) - jnp.exp(s23) + lambda_init
    a0, a1 = acc_ref[0], acc_ref[1]
    ot = a0[:LANES] / a0[LANES:LANES + 1] - lam * (a1[:LANES] / a1[LANES:LANES + 1])
    o = ot.T
    o = o * lax.rsqrt(jnp.mean(o * o, axis=-1, keepdims=True) + NORM_EPS)
    o_ref[...] = (o * g_ref[...] * (1.0 - lambda_init)).astype(BF16)


def diff_attention(proj, lamv, subln_g, slopes, *, lambda_init, tq=1024, tk=512):
    t = proj.shape[0]
    nk = t // tk
    assert tq == 2 * tk and t % tq == 0
    cb = lambda col: col // LANES
    return pl.pallas_call(
        functools.partial(_diff_attn_kernel, tq=tq, tk=tk, nk=nk, lambda_init=lambda_init),
        out_shape=jax.ShapeDtypeStruct((t, DIFF_HEADS * LANES), BF16),
        grid=(DIFF_HEADS, t // tq),
        in_specs=[
            pl.BlockSpec(memory_space=pltpu.SMEM),
            pl.BlockSpec((tq, LANES), lambda h, i: (i, cb(COL_QA) + h)),
            pl.BlockSpec((t, LANES), lambda h, i: (0, cb(COL_KA) + h)),
            pl.BlockSpec((t, LANES), lambda h, i: (0, cb(COL_VA) + h)),
            pl.BlockSpec((4, DIFF_QK_DIM), lambda h, i: (0, 0)),
            pl.BlockSpec((1, LANES), lambda h, i: (0, 0)),
        ],
        out_specs=pl.BlockSpec((tq, LANES), lambda h, i: (i, h)),
        scratch_shapes=[
            pltpu.VMEM((2, t, LANES), BF16),
            pltpu.VMEM((nk, LANES + ONES_ROWS, tk), BF16),
            pltpu.VMEM((2, tq, LANES), BF16),
            pltpu.VMEM((2, tk, tq), F32),
            pltpu.VMEM((2, tk, tq), F32),
            pltpu.VMEM((2, 1, tq), F32),
            pltpu.VMEM((2, LANES + ONES_ROWS, tq), F32),
        ],
        compiler_params=_cparams(("arbitrary", "arbitrary")),
        name="diff_attention",
    )(slopes, proj, proj, proj, lamv, subln_g.reshape(1, LANES))


def _swa_kernel(sink_ref, qa_ref, qb_ref, k_ref, v_ref, kh_ref, vh_ref, o_ref, *, tq, slopes):
    i = pl.program_id(0)
    w = SWA_WINDOW
    low = lax.broadcasted_iota(jnp.int32, (w, LANES), 1) < SWA_HEAD_DIM
    low2 = lax.broadcasted_iota(jnp.int32, (2 * w, LANES), 1) < SWA_HEAD_DIM
    row = lax.broadcasted_iota(jnp.int32, (w, 2 * w), 0)
    col = lax.broadcasted_iota(jnp.int32, (w, 2 * w), 1)
    dist = row + w - col
    valid = jnp.logical_and(dist >= 0, dist < w)
    valid0 = jnp.logical_and(valid, jnp.logical_or(col >= w, i > 0))
    distf = dist.astype(F32)
    q_refs = (qa_ref, qb_ref)

    def one_head(qh, kd, vd, ok, hq):
        s = _dot_nt(qh, kd)
        yield
        s = jnp.where(ok, s - slopes[hq] * distf, NEG_INF)
        sink = sink_ref[hq]
        m = jnp.maximum(jnp.max(s, axis=-1, keepdims=True), sink)
        yield
        e = jnp.exp(s - m)
        den = jnp.sum(e, axis=-1, keepdims=True) + jnp.exp(sink - m)
        yield
        return _dot((e / den).astype(BF16), vd)

    chains, where_to = [], []
    for r in range(tq // w):
        rs = slice(r * w, (r + 1) * w)
        ok = valid0 if r == 0 else valid
        if r == 0:
            kband = jnp.concatenate([kh_ref[...], k_ref[0:w, :]], axis=0)
            vband = jnp.concatenate([vh_ref[...], v_ref[0:w, :]], axis=0)
        else:
            kband = k_ref[(r - 1) * w:(r + 1) * w, :]
            vband = v_ref[(r - 1) * w:(r + 1) * w, :]
        kroll = pltpu.roll(kband, SWA_HEAD_DIM, axis=1)
        vroll = pltpu.roll(vband, SWA_HEAD_DIM, axis=1)
        for kv in range(2):
            kd = (jnp.where(low2, kband, kroll) if kv == 0 else jnp.where(low2, kroll, kband))
            vd = (jnp.where(low2, vband, vroll) if kv == 0 else jnp.where(low2, vroll, vband))
            kd = kd.astype(BF16)
            vd = vd.astype(BF16)
            for j in range(2):
                jj = kv * 2 + j
                q = q_refs[kv][rs, j * LANES:(j + 1) * LANES] * (SWA_HEAD_DIM ** -0.5)
                for half in range(2):
                    qh = jnp.where(low, q, 0.0) if half == 0 else jnp.where(low, 0.0, q)
                    chains.append(one_head(qh.astype(BF16), kd, vd, ok, 2 * jj + half))
                where_to.append((rs, jj))
    outs = _interleave(chains)
    for n, (rs, jj) in enumerate(where_to):
        o_ref[rs, jj * LANES:(jj + 1) * LANES] = jnp.where(low, outs[2 * n], outs[2 * n + 1]).astype(BF16)


def sliding_window_attention(proj, sinks, slopes, *, tq=512):
    t = proj.shape[0]
    w = SWA_WINDOW
    qw = SWA_Q_HEADS * SWA_HEAD_DIM
    rpb = tq // w
    qblk = COL_QB // (2 * LANES)
    kblk, vblk = COL_KB // LANES, COL_VB // LANES
    return pl.pallas_call(
        functools.partial(_swa_kernel, tq=tq, slopes=tuple(float(s) for s in slopes)),
        out_shape=jax.ShapeDtypeStruct((t, qw), BF16),
        grid=(t // tq,),
        in_specs=[
            pl.BlockSpec(memory_space=pltpu.SMEM),
            pl.BlockSpec((tq, 2 * LANES), lambda i: (i, qblk)),
            pl.BlockSpec((tq, 2 * LANES), lambda i: (i, qblk + 1)),
            pl.BlockSpec((tq, LANES), lambda i: (i, kblk)),
            pl.BlockSpec((tq, LANES), lambda i: (i, vblk)),
            pl.BlockSpec((w, LANES), lambda i: (jnp.maximum(i * rpb - 1, 0), kblk)),
            pl.BlockSpec((w, LANES), lambda i: (jnp.maximum(i * rpb - 1, 0), vblk)),
        ],
        out_specs=pl.BlockSpec((tq, qw), lambda i: (i, 0)),
        compiler_params=_cparams(("parallel",)),
        name="swa_attention",
    )(sinks, proj, proj, proj, proj, proj, proj)


def _interleave(gens):
    gens = list(gens)
    results = [None] * len(gens)
    live = list(range(len(gens)))
    while live:
        still = []
        for idx in live:
            try:
                next(gens[idx])
                still.append(idx)
            except StopIteration as done:
                results[idx] = done.value
        live = still
    return results


def _seg_sum(x, low):
    s0 = jnp.sum(jnp.where(low, x, 0.0), axis=-1, keepdims=True)
    s1 = jnp.sum(jnp.where(low, 0.0, x), axis=-1, keepdims=True)
    return jnp.where(low, s0, s1)


def _stack_heads(z, low):
    zero = jnp.zeros_like(z)
    return jnp.concatenate([jnp.where(low, z, zero), jnp.where(low, zero, z)], axis=0)


def _round_robin(subs, results):
    live = list(range(len(subs)))
    while live:
        still = []
        for idx in live:
            try:
                next(subs[idx])
                still.append(idx)
            except StopIteration as done:
                results[idx] = done.value
        live = still
        yield


def _rwkv_kernel(*refs, tt, grp):
    nb = grp * LANES // RWKV_BLOCK
    r_refs, k_refs, v_refs = refs[0:nb], refs[nb:2 * nb], refs[2 * nb:3 * nb]
    lo_refs = refs[3 * nb:3 * nb + 4]
    _rwkv_body(r_refs, k_refs, v_refs, lo_refs, *refs[3 * nb + 4:], tt=tt, grp=grp)


def _rwkv_body(r_refs, k_refs, v_refs, lo_refs, pp_ref, w2_ref, a2_ref, g2_ref, y_ref, h_ref, *,
               tt, grp):
    i = pl.program_id(1)
    nch = tt // CHUNK
    c2 = 2 * CHUNK
    gw = grp * LANES

    @pl.when(i == 0)
    def _():
        h_ref[...] = jnp.zeros(h_ref.shape, F32)

    pp = pp_ref[...]
    w0, a0, k_k, k_a, r_k, ln_w, ln_b = (pp[0:1], pp[1:2], pp[2:3], pp[3:4], pp[4:5], pp[5:6],
                                         pp[6:7])
    low_c = lax.broadcasted_iota(jnp.int32, (CHUNK, LANES), 1) < RWKV_HEAD_DIM
    rr = lax.broadcasted_iota(jnp.int32, (CHUNK, LANES), 0)
    cc = lax.broadcasted_iota(jnp.int32, (CHUNK, LANES), 1) & (RWKV_HEAD_DIM - 1)
    lt = rr > cc
    le = rr >= cc
    eye = jnp.where(rr == cc, 1.0, 0.0)
    bd = (lax.broadcasted_iota(jnp.int32, (c2, LANES), 0) < CHUNK) == (
        lax.broadcasted_iota(jnp.int32, (c2, LANES), 1) < RWKV_HEAD_DIM)
    rowc = lax.broadcasted_iota(jnp.int32, (CHUNK, gw), 0)

    def seg_sum(x):
        return jnp.concatenate(
            [_seg_sum(x[:, p * LANES:(p + 1) * LANES], low_c) for p in range(grp)], axis=1)

    def prepare_chain(at, rt, bt, kt, vs, bh, khat, etot):
        sb = _stack_heads(bt, low_c)
        sk = _stack_heads(kt, low_c)
        sv = _stack_heads(vs, low_c)
        a4 = _dot_nt(jnp.concatenate([at, rt], axis=0), jnp.concatenate([sb, sk], axis=0))
        kv = jnp.where(bd, _dot(khat.T.astype(BF16), vs), 0.0)
        bht = bh.T.astype(BF16)
        dx = jnp.broadcast_to(etot, (c2, LANES)).T
        yield
        n = jnp.where(lt, a4[:CHUNK, :LANES], 0.0)
        aak = jnp.where(lt, a4[:CHUNK, LANES:], 0.0)
        arb = jnp.where(le, a4[CHUNK:, :LANES], 0.0)
        ark = jnp.where(le, a4[CHUNK:, LANES:], 0.0)
        av = _dot(jnp.concatenate([aak, ark], axis=0).astype(BF16), sv)
        tinv = eye + n
        pw = n
        for _ in range(int(math.log2(CHUNK)) - 1):
            pb = pw.astype(BF16)
            pw = _dot(pb, _stack_heads(pb, low_c))
            yield
            tinv = tinv + _dot(pw.astype(BF16), _stack_heads(tinv.astype(BF16), low_c))
            yield
        rhs = jnp.concatenate([_stack_heads(at, low_c),
                               _stack_heads(av[:CHUNK].astype(BF16), low_c)], axis=1)
        wu = _dot(tinv.astype(BF16), rhs)
        yield
        return (jnp.concatenate([wu[:, :LANES].astype(BF16), rt], axis=0), wu[:, LANES:],
                av[CHUNK:], arb.astype(BF16), bht, kv, dx)

    def chunk_chain(c):
        rows = slice(c * CHUNK, (c + 1) * CHUNK)
        lo0, lo1, lo2, lo3 = (lo_refs[n][rows, :] for n in range(4))
        z = -(w0 + _dot(jnp.tanh(lo0).astype(BF16), w2_ref[...]))
        a_pre = a0 + _dot(jnp.concatenate([lo0, lo1], axis=1).astype(BF16), a2_ref[...])
        gate_in = jnp.concatenate([lo1, lo2, lo3], axis=1)
        g = _dot((1.0 / (1.0 + jnp.exp(-gate_in))).astype(BF16), g2_ref[...])
        yield
        r, k, v = (jnp.concatenate([ref[rows, :] for ref in refs_], axis=1)
                   for refs_ in (r_refs, k_refs, v_refs))
        softplus = jnp.maximum(z, 0.0) + jnp.log(1.0 + jnp.exp(-jnp.abs(z)))
        lw = -jnp.exp(-softplus - 0.5)
        a = 1.0 / (1.0 + jnp.exp(-a_pre))
        kx = k * k_k
        kk = kx * lax.rsqrt(jnp.maximum(seg_sum(kx * kx), 1e-24))
        k2 = k * (1.0 + (a - 1.0) * k_a)
        bvec = kk * a
        cum = lw
        for s in (1, 2, 4, 8, 16, 32):
            cum = cum + jnp.where(rowc >= s, pltpu.roll(cum, s, axis=0), 0.0)
        tot = cum[CHUNK - 1:CHUNK, :]
        einv = jnp.exp(-cum)
        erem = jnp.exp(tot - cum)
        at = (-kk * jnp.exp(cum - lw)).astype(BF16)
        rt = (r * jnp.exp(cum)).astype(BF16)
        bt = (bvec * einv).astype(BF16)
        kt = (k2 * einv).astype(BF16)
        bh = bvec * erem
        khat = k2 * erem
        vs = v.astype(BF16)
        etot = jnp.exp(tot)
        bonus = seg_sum(r * k2 * r_k) * v
        yield
        pairs = [None] * grp
        subs = []
        for p in range(grp):
            ln = slice(p * LANES, (p + 1) * LANES)
            subs.append(prepare_chain(at[:, ln], rt[:, ln], bt[:, ln], kt[:, ln], vs[:, ln],
                                      bh[:, ln], khat[:, ln], etot[:, ln]))
        yield from _round_robin(subs, pairs)
        return pairs, g, bonus

    hs = [h_ref[p] for p in range(grp)]

    def advance_chain(c, pairs, g, bonus):
        rows = slice(c * CHUNK, (c + 1) * CHUNK)
        whs = [_dot(pairs[p][0], hs[p].astype(BF16)) for p in range(grp)]
        yield
        ys = []
        for p in range(grp):
            _, u0, y0, arb, bht, kv, dx = pairs[p]
            u = (whs[p][:CHUNK] + u0).astype(BF16)
            ys.append(whs[p][CHUNK:] + _dot(arb, _stack_heads(u, low_c)) + y0)
            hs[p] = dx * hs[p] + jnp.where(bd, _dot(bht, u), 0.0) + kv
        yield
        y = jnp.concatenate(ys, axis=1)
        mean = seg_sum(y) * (1.0 / RWKV_HEAD_DIM)
        yc = y - mean
        var = seg_sum(yc * yc) * (1.0 / RWKV_HEAD_DIM)
        yn = yc * lax.rsqrt(var + RWKV_GN_EPS) * ln_w + ln_b
        y_ref[rows, :] = ((yn + bonus) * g).astype(BF16)

    chains = [chunk_chain(c) for c in range(nch)]
    ready = {}
    active, started, rnd = [], 0, 0
    adv, adv_next = None, 0
    while adv_next < nch:
        if started < nch and rnd % CHUNK_STAGGER == 0:
            active.append(started)
            started += 1
        for c in list(active):
            try:
                next(chains[c])
            except StopIteration as done:
                ready[c] = done.value
                active.remove(c)
        if adv is None and adv_next in ready:
            adv = advance_chain(adv_next, *ready.pop(adv_next))
        if adv is not None:
            try:
                next(adv)
            except StopIteration:
                adv, adv_next = None, adv_next + 1
        rnd += 1

    for p in range(grp):
        h_ref[p] = hs[p]


def rwkv7_mix(proj, pp, w2p, a2p, g2p, layer, *, tt=512, grp=RWKV_GROUP):
    t = proj.shape[0]
    c2 = 2 * CHUNK
    gw = grp * LANES
    nb = gw // RWKV_BLOCK

    def col_blocks(col, width, count, per_group):
        return [pl.BlockSpec((tt, width), lambda p, i, o=col // width + n: (i, o + per_group * p))
                for n in range(count)]

    in_specs = (col_blocks(COL_R, RWKV_BLOCK, nb, nb) + col_blocks(COL_K, RWKV_BLOCK, nb, nb)
                + col_blocks(COL_V, RWKV_BLOCK, nb, nb) + col_blocks(COL_LORA, LANES, 4, 0))
    in_specs += [
        pl.BlockSpec((None, SUBLANES, gw), lambda p, i: (layer, 0, p)),
        pl.BlockSpec((None, LANES, gw), lambda p, i: (layer, 0, p)),
        pl.BlockSpec((None, 2 * LANES, gw), lambda p, i: (layer, 0, p)),
        pl.BlockSpec((None, 3 * LANES, gw), lambda p, i: (layer, 0, p)),
    ]
    return pl.pallas_call(
        functools.partial(_rwkv_kernel, tt=tt, grp=grp),
        out_shape=jax.ShapeDtypeStruct((t, RWKV_WIDTH), BF16),
        grid=(RWKV_PAIRS // grp, t // tt),
        in_specs=in_specs,
        out_specs=pl.BlockSpec((tt, gw), lambda p, i: (i, p)),
        scratch_shapes=[pltpu.VMEM((grp, c2, LANES), F32)],
        compiler_params=_cparams(("arbitrary", "arbitrary")),
        name="rwkv7_mix",
    )(*([proj] * (3 * nb + 4)), pp, w2p, a2p, g2p)


def _alibi_slopes():
    idx = np.arange(1, ALIBI_HEADS + 1, dtype=np.float32)
    m = np.exp2(-8.0 * idx / ALIBI_HEADS).astype(np.float32)
    diff_idx = np.arange(2, ALIBI_HEADS, 3)
    swa_idx = np.setdiff1d(np.arange(ALIBI_HEADS), diff_idx)
    return m[diff_idx], m[swa_idx]


def _pad_rows(w, before, total):
    return jnp.pad(w, ((0, 0), (before, total - before - w.shape[1]), (0, 0)))


def kernel(x, attn_norm_g, w_in, diff_lambda, diff_subln_g, swa_sinks, rwkv_mu, rwkv_w0, rwkv_w2,
           rwkv_a0, rwkv_a2, rwkv_g2, rwkv_k_k, rwkv_k_a, rwkv_r_k, rwkv_ln_w, rwkv_ln_b, w_out,
           ffn_norm_g, w_gate_up, w_down, final_norm_g):
    b, t, d = x.shape
    depth = w_in.shape[0]
    diff_slopes, swa_slopes = _alibi_slopes()
    diff_slopes = jnp.asarray(diff_slopes)

    w_in_b = cast_pad_cols(w_in, PROJ_COLS)
    w_down_b = w_down.astype(BF16)
    mu_cols = jnp.pad(rwkv_mu, ((0, 0), (IN_COLS - rwkv_mu.shape[1], PROJ_COLS - IN_COLS))).reshape(
        depth, 1, PROJ_COLS)
    w2_b = _pad_rows(rwkv_w2, 0, LANES).astype(BF16)
    a2_b = _pad_rows(rwkv_a2, DECAY_LORA, 2 * LANES).astype(BF16)
    g2_b = _pad_rows(rwkv_g2, DECAY_LORA + ICLR_LORA - LANES, 3 * LANES).astype(BF16)
    pp = jnp.stack([rwkv_w0, rwkv_a0, rwkv_k_k, rwkv_k_a, rwkv_r_k.reshape(depth, RWKV_WIDTH),
                    rwkv_ln_w, rwkv_ln_b, jnp.zeros_like(rwkv_w0)], axis=1)
    attn_g = attn_norm_g.reshape(depth, 1, d)
    ffn_g = ffn_norm_g.reshape(depth, 1, d)

    outs = []
    for bi in range(b):
        xb = x[bi]
        for l in range(depth):
            lambda_init = 0.8 - 0.6 * math.exp(-0.3 * l)
            proj = norm_matmul(xb, attn_g, w_in_b, mu_cols, l)
            ya = diff_attention(proj, diff_lambda[l], diff_subln_g[l], diff_slopes,
                                lambda_init=lambda_init)
            yb = sliding_window_attention(proj, swa_sinks[l], swa_slopes)
            yc = rwkv7_mix(proj, pp, w2_b, a2_b, g2_b, l)
            xb = matmul_residual([ya, yb, yc], w_out, l, xb, tn=1024, name="out_proj")
            hid = ffn_up(xb, ffn_g, w_gate_up, l)
            xb = matmul_residual([hid], w_down_b, l, xb, name="ffn_down")
        outs.append(rmsnorm(xb, final_norm_g))
    return jnp.stack(outs, axis=0)
```

```python
import functools
import math

import numpy as np
import jax
import jax.numpy as jnp
from jax import lax
from jax.experimental import pallas as pl
from jax.experimental.pallas import tpu as pltpu

F32 = jnp.float32
BF16 = jnp.bfloat16

D_MODEL = 2048
LANES = 128
SUBLANES = 8
DIFF_HEADS = 4
DIFF_QK_DIM = 64
SWA_Q_HEADS = 8
SWA_HEAD_DIM = 64
SWA_WINDOW = 128
RWKV_HEADS = 16
RWKV_HEAD_DIM = 64
RWKV_WIDTH = RWKV_HEADS * RWKV_HEAD_DIM
RWKV_PAIRS = RWKV_WIDTH // LANES
DECAY_LORA = 96
ICLR_LORA = 96
GATE_LORA = 256
FFN_HIDDEN = 5632
ALIBI_HEADS = 12
NORM_EPS = 1e-5
RWKV_GN_EPS = 64e-5
NEG_INF = -1e30
CHUNK = 64
RWKV_GROUP = 8
CHUNK_STAGGER = 3

IN_COLS = 5824
COL_QA, COL_KA, COL_VA = 0, 512, 1024
COL_QB, COL_KB, COL_VB = 1536, 2048, 2176
COL_R, COL_K, COL_V = 2304, 3328, 4352
COL_LORA = 5376
PROJ_COLS = 6144
LORA_COLS = 512
RWKV_BLOCK = 2 * LANES
NORM_CHUNKS = 4

VMEM_LIMIT = 56 * 1024 * 1024


def _cparams(sem):
    return pltpu.CompilerParams(dimension_semantics=sem, vmem_limit_bytes=VMEM_LIMIT)


def _dot(a, b):
    return jnp.dot(a, b, preferred_element_type=F32)


def _dot_nt(a, b):
    return lax.dot_general(a, b, (((1,), (1,)), ((), ())), preferred_element_type=F32)


def _norm_rows(x, g):
    ms = jnp.mean(x * x, axis=-1, keepdims=True)
    return (x * lax.rsqrt(ms + NORM_EPS) * g).astype(BF16)


def _norm_then_dots(x_ref, g_ref, h_ref, weights):
    tm = x_ref.shape[0]
    cm = tm // NORM_CHUNKS
    g = g_ref[...]
    parts = []
    for c in range(NORM_CHUNKS):
        rows = slice(c * cm, (c + 1) * cm)
        h = _norm_rows(x_ref[rows, :], g)
        h_ref[rows, :] = h
        parts.append([_dot(h, w) for w in weights])
    return [jnp.concatenate([p[k] for p in parts], axis=0) for k in range(len(weights))]


def _norm_mm_kernel(x_ref, g_ref, w_ref, mu_ref, o_ref, h_ref, last_ref):
    i = pl.program_id(0)
    j = pl.program_id(1)
    w = w_ref[...]

    def finish(acc):
        rowi = lax.broadcasted_iota(jnp.int32, acc.shape, 0)
        first = jnp.where(i > 0, last_ref[j, SUBLANES - 1:SUBLANES, :], 0.0)
        prev = jnp.where(rowi == 0, first, pltpu.roll(acc, 1, axis=0))
        last_ref[j] = acc[acc.shape[0] - SUBLANES:, :]
        o_ref[...] = acc + (prev - acc) * mu_ref[...]

    @pl.when(j == 0)
    def _():
        finish(_norm_then_dots(x_ref, g_ref, h_ref, [w])[0])

    @pl.when(j > 0)
    def _():
        finish(_dot(h_ref[...], w))


def norm_matmul(x, g, w, mu, layer, *, tm=1024, tn=1024):
    t, d = x.shape
    n = w.shape[2]
    nj = pl.cdiv(n, tn)
    return pl.pallas_call(
        _norm_mm_kernel,
        out_shape=jax.ShapeDtypeStruct((t, n), F32),
        grid=(t // tm, nj),
        in_specs=[
            pl.BlockSpec((tm, d), lambda i, j: (i, 0)),
            pl.BlockSpec((None, 1, d), lambda i, j: (layer, 0, 0)),
            pl.BlockSpec((None, d, tn), lambda i, j: (layer, 0, j)),
            pl.BlockSpec((None, 1, tn), lambda i, j: (layer, 0, j)),
        ],
        out_specs=pl.BlockSpec((tm, tn), lambda i, j: (i, j)),
        scratch_shapes=[pltpu.VMEM((tm, d), BF16), pltpu.VMEM((nj, SUBLANES, tn), F32)],
        compiler_params=_cparams(("arbitrary", "arbitrary")),
        name="norm_in_proj",
    )(x, g, w, mu)


def _ffn_up_kernel(x_ref, g_ref, wg_ref, wu_ref, o_ref, h_ref):
    j = pl.program_id(1)
    wg = wg_ref[...].astype(BF16)
    wu = wu_ref[...].astype(BF16)

    def finish(gate, up):
        o_ref[...] = (gate / (1.0 + jnp.exp(-gate)) * up).astype(BF16)

    @pl.when(j == 0)
    def _():
        finish(*_norm_then_dots(x_ref, g_ref, h_ref, [wg, wu]))

    @pl.when(j > 0)
    def _():
        h = h_ref[...]
        finish(_dot(h, wg), _dot(h, wu))


def ffn_up(x, g, w_gate_up, layer, *, tm=1024, tn=512):
    t, d = x.shape
    hid = w_gate_up.shape[2] // 2
    nj = hid // tn
    return pl.pallas_call(
        _ffn_up_kernel,
        out_shape=jax.ShapeDtypeStruct((t, hid), BF16),
        grid=(t // tm, nj),
        in_specs=[
            pl.BlockSpec((tm, d), lambda i, j: (i, 0)),
            pl.BlockSpec((None, 1, d), lambda i, j: (layer, 0, 0)),
            pl.BlockSpec((None, d, tn), lambda i, j: (layer, 0, j)),
            pl.BlockSpec((None, d, tn), lambda i, j: (layer, 0, j + nj)),
        ],
        out_specs=pl.BlockSpec((tm, tn), lambda i, j: (i, j)),
        scratch_shapes=[pltpu.VMEM((tm, d), BF16)],
        compiler_params=_cparams(("parallel", "arbitrary")),
        name="ffn_up",
    )(x, g, w_gate_up, w_gate_up)


def _mm_res_kernel(*refs, splits):
    n = len(splits)
    a_refs, (w_ref, x_ref, o_ref) = refs[:n], refs[n:]
    acc = x_ref[...]
    off = 0
    for a_ref, k in zip(a_refs, splits):
        acc = acc + _dot(a_ref[...], w_ref[off:off + k, :].astype(BF16))
        off += k
    o_ref[...] = acc


def matmul_residual(a_list, w, layer, x, *, tm=1024, tn=512, name):
    t, n = x.shape
    splits = tuple(a.shape[1] for a in a_list)
    k = sum(splits)
    in_specs = [pl.BlockSpec((tm, s), lambda i, j: (i, 0)) for s in splits]
    in_specs += [
        pl.BlockSpec((None, k, tn), lambda i, j: (layer, 0, j)),
        pl.BlockSpec((tm, tn), lambda i, j: (i, j)),
    ]
    return pl.pallas_call(
        functools.partial(_mm_res_kernel, splits=splits),
        out_shape=jax.ShapeDtypeStruct((t, n), F32),
        grid=(t // tm, n // tn),
        in_specs=in_specs,
        out_specs=pl.BlockSpec((tm, tn), lambda i, j: (i, j)),
        compiler_params=_cparams(("parallel", "arbitrary")),
        name=name,
    )(*a_list, w, x)


def _rmsnorm_kernel(x_ref, g_ref, o_ref):
    x = x_ref[...]
    ms = jnp.mean(x * x, axis=-1, keepdims=True)
    o_ref[...] = x * lax.rsqrt(ms + NORM_EPS) * g_ref[...]


def rmsnorm(x, g, *, tm=512):
    t, d = x.shape
    return pl.pallas_call(
        _rmsnorm_kernel,
        out_shape=jax.ShapeDtypeStruct((t, d), F32),
        grid=(t // tm,),
        in_specs=[pl.BlockSpec((tm, d), lambda i: (i, 0)), pl.BlockSpec((1, d), lambda i: (0, 0))],
        out_specs=pl.BlockSpec((tm, d), lambda i: (i, 0)),
        compiler_params=_cparams(("parallel",)),
        name="final_rmsnorm",
    )(x, g.reshape(1, d))


LOG2E = 1.4426950408889634
POS_SPLIT = 128
ONES_ROWS = 16


def _pos_lanes(lane, base, cols):
    out = jnp.zeros(lane.shape, F32)
    for idx, col in enumerate(cols):
        out = jnp.where(lane == base + idx, col, out)
    return out


def _diff_attn_kernel(slope_ref, q_ref, k_ref, v_ref, lam_ref, g_ref, o_ref,
                      kb_ref, vt_ref, qs_ref, sa_ref, sb_ref, m_ref, acc_ref, *, tq, tk, nk,
                      lambda_init):
    h = pl.program_id(0)
    i = pl.program_id(1)
    kpq = tq // tk
    beta = jnp.full((1, LANES), slope_ref[h] * LOG2E, F32)
    b1 = beta.astype(BF16).astype(F32)
    b2 = (beta - b1).astype(BF16).astype(F32)
    b3 = (beta - b1 - b2).astype(BF16).astype(F32)
    own = [lambda lane, c=c: (lane < DIFF_QK_DIM) == (c == 0) for c in range(2)]
    pos_base = [DIFF_QK_DIM, 0]

    @pl.when(i == 0)
    def _():
        lane = lax.broadcasted_iota(jnp.int32, (tk, LANES), 1)
        krel = lax.broadcasted_iota(jnp.int32, (tk, LANES), 0)
        khi = ((krel // POS_SPLIT) * POS_SPLIT).astype(F32)
        klo = (krel % POS_SPLIT).astype(F32)
        kcols = []
        for b in (b1, b2, b3):
            kcols += [-b * POS_SPLIT, -b, khi, klo]
        posk = [_pos_lanes(lane, pos_base[c], kcols) for c in range(2)]
        ones = jnp.where(lax.broadcasted_iota(jnp.int32, (ONES_ROWS, tk), 0) == 0, 1.0, 0.0)

        def stage(j, carry):
            rows = pl.ds(pl.multiple_of(j * tk, tk), tk)
            k = k_ref[rows, :]
            for c in range(2):
                kb_ref[c, rows, :] = jnp.where(own[c](lane), k, posk[c]).astype(BF16)
            vt_ref[j] = jnp.concatenate([v_ref[rows, :].T, ones], axis=0).astype(BF16)
            return carry

        lax.fori_loop(0, nk, stage, 0)

    q = q_ref[...] * (DIFF_QK_DIM ** -0.5 * LOG2E)
    lane = lax.broadcasted_iota(jnp.int32, (tq, LANES), 1)
    qrel = lax.broadcasted_iota(jnp.int32, (tq, LANES), 0)
    qhi = (qrel // POS_SPLIT).astype(F32)
    qlo = (qrel % POS_SPLIT).astype(F32)
    qcols = []
    for b in (b1, b2, b3):
        qcols += [qhi, qlo, b, b]
    for c in range(2):
        qs_ref[c] = jnp.where(own[c](lane), q, _pos_lanes(lane, pos_base[c], qcols)).astype(BF16)
    m_ref[...] = jnp.full(m_ref.shape, NEG_INF, F32)
    acc_ref[...] = jnp.zeros(acc_ref.shape, F32)

    def scores(j, dst_ref, qlo):
        rows = pl.ds(pl.multiple_of(j * tk, tk), tk)

        def chain(c):
            s = _dot_nt(kb_ref[c, rows, :], qs_ref[c, qlo:, :])
            yield
            return s

        return [chain(c) for c in range(2)], dst_ref

    def consume(src_ref, j, qlo, masked):
        vt = vt_ref[j]
        c0 = (slope_ref[h] * LOG2E) * (j * tk - i * tq).astype(F32)

        def chain(c):
            s = src_ref[c, :, qlo:]
            if masked:
                key = lax.broadcasted_iota(jnp.int32, s.shape, 0)
                qry = lax.broadcasted_iota(jnp.int32, s.shape, 1)
                s = jnp.where(qry >= key, s, NEG_INF)
            m_prev = m_ref[c, :, qlo:]
            m_new = jnp.maximum(m_prev, jnp.max(s, axis=0, keepdims=True) + c0)
            yield
            p = jnp.exp2(s - (m_new - c0)).astype(BF16)
            alpha = jnp.exp2(m_prev - m_new)
            yield
            return alpha * acc_ref[c, :, qlo:] + _dot(vt, p), m_new

        return [chain(c) for c in range(2)]

    def step(score_args, consume_args):
        chains = consume(*consume_args)
        dst, dst_qlo = None, 0
        if score_args is not None:
            score_chains, dst = scores(*score_args)
            dst_qlo = score_args[2]
            chains = chains + score_chains
        results = _interleave(chains)
        qlo = consume_args[2]
        for c in range(2):
            acc_new, m_new = results[c]
            acc_ref[c, :, qlo:] = acc_new
            m_ref[c, :, qlo:] = m_new
            if dst is not None:
                dst[c, :, dst_qlo:] = results[2 + c]

    for c, s in enumerate(_interleave(scores(0, sa_ref, 0)[0])):
        sa_ref[c] = s

    def body(jj, carry):
        j = jj * kpq
        step((j + 1, sb_ref, 0), (sa_ref, j, 0, False))
        step((j + 2, sa_ref, 0), (sb_ref, j + 1, 0, False))
        return carry

    lax.fori_loop(0, i, body, 0)
    step((i * kpq + 1, sb_ref, tk), (sa_ref, i * kpq, 0, True))
    step(None, (sb_ref, i * kpq + 1, tk, True))

    lamv = lam_ref[...]
    s01 = jnp.sum(lamv[0:1] * lamv[1:2], axis=-1, keepdims=True)
    s23 = jnp.sum(lamv[2:3] * lamv[3:4], axis=-1, keepdims=True)
    lam = jnp.exp(s01) - jnp.exp(s23) + lambda_init
    a0, a1 = acc_ref[0], acc_ref[1]
    ot = a0[:LANES] / a0[LANES:LANES + 1] - lam * (a1[:LANES] / a1[LANES:LANES + 1])
    o = ot.T
    o = o * lax.rsqrt(jnp.mean(o * o, axis=-1, keepdims=True) + NORM_EPS)
    o_ref[...] = (o * g_ref[...] * (1.0 - lambda_init)).astype(BF16)


def diff_attention(proj, lamv, subln_g, slopes, *, lambda_init, tq=1024, tk=512):
    t = proj.shape[0]
    nk = t // tk
    assert tq == 2 * tk and t % tq == 0
    cb = lambda col: col // LANES
    return pl.pallas_call(
        functools.partial(_diff_attn_kernel, tq=tq, tk=tk, nk=nk, lambda_init=lambda_init),
        out_shape=jax.ShapeDtypeStruct((t, DIFF_HEADS * LANES), BF16),
        grid=(DIFF_HEADS, t // tq),
        in_specs=[
            pl.BlockSpec(memory_space=pltpu.SMEM),
            pl.BlockSpec((tq, LANES), lambda h, i: (i, cb(COL_QA) + h)),
            pl.BlockSpec((t, LANES), lambda h, i: (0, cb(COL_KA) + h)),
            pl.BlockSpec((t, LANES), lambda h, i: (0, cb(COL_VA) + h)),
            pl.BlockSpec((4, DIFF_QK_DIM), lambda h, i: (0, 0)),
            pl.BlockSpec((1, LANES), lambda h, i: (0, 0)),
        ],
        out_specs=pl.BlockSpec((tq, LANES), lambda h, i: (i, h)),
        scratch_shapes=[
            pltpu.VMEM((2, t, LANES), BF16),
            pltpu.VMEM((nk, LANES + ONES_ROWS, tk), BF16),
            pltpu.VMEM((2, tq, LANES), BF16),
            pltpu.VMEM((2, tk, tq), F32),
            pltpu.VMEM((2, tk, tq), F32),
            pltpu.VMEM((2, 1, tq), F32),
            pltpu.VMEM((2, LANES + ONES_ROWS, tq), F32),
        ],
        compiler_params=_cparams(("arbitrary", "arbitrary")),
        name="diff_attention",
    )(slopes, proj, proj, proj, lamv, subln_g.reshape(1, LANES))


def _swa_kernel(sink_ref, qa_ref, qb_ref, k_ref, v_ref, kh_ref, vh_ref, o_ref, *, tq, slopes):
    i = pl.program_id(0)
    w = SWA_WINDOW
    low = lax.broadcasted_iota(jnp.int32, (w, LANES), 1) < SWA_HEAD_DIM
    low2 = lax.broadcasted_iota(jnp.int32, (2 * w, LANES), 1) < SWA_HEAD_DIM
    row = lax.broadcasted_iota(jnp.int32, (w, 2 * w), 0)
    col = lax.broadcasted_iota(jnp.int32, (w, 2 * w), 1)
    dist = row + w - col
    valid = jnp.logical_and(dist >= 0, dist < w)
    valid0 = jnp.logical_and(valid, jnp.logical_or(col >= w, i > 0))
    distf = dist.astype(F32)
    q_refs = (qa_ref, qb_ref)

    def one_head(qh, kd, vd, ok, hq):
        s = _dot_nt(qh, kd)
        yield
        s = jnp.where(ok, s - slopes[hq] * distf, NEG_INF)
        sink = sink_ref[hq]
        m = jnp.maximum(jnp.max(s, axis=-1, keepdims=True), sink)
        yield
        e = jnp.exp(s - m)
        den = jnp.sum(e, axis=-1, keepdims=True) + jnp.exp(sink - m)
        yield
        return _dot((e / den).astype(BF16), vd)

    chains, where_to = [], []
    for r in range(tq // w):
        rs = slice(r * w, (r + 1) * w)
        ok = valid0 if r == 0 else valid
        if r == 0:
            kband = jnp.concatenate([kh_ref[...], k_ref[0:w, :]], axis=0)
            vband = jnp.concatenate([vh_ref[...], v_ref[0:w, :]], axis=0)
        else:
            kband = k_ref[(r - 1) * w:(r + 1) * w, :]
            vband = v_ref[(r - 1) * w:(r + 1) * w, :]
        kroll = pltpu.roll(kband, SWA_HEAD_DIM, axis=1)
        vroll = pltpu.roll(vband, SWA_HEAD_DIM, axis=1)
        for kv in range(2):
            kd = (jnp.where(low2, kband, kroll) if kv == 0 else jnp.where(low2, kroll, kband))
            vd = (jnp.where(low2, vband, vroll) if kv == 0 else jnp.where(low2, vroll, vband))
            kd = kd.astype(BF16)
            vd = vd.astype(BF16)
            for j in range(2):
                jj = kv * 2 + j
                q = q_refs[kv][rs, j * LANES:(j + 1) * LANES] * (SWA_HEAD_DIM ** -0.5)
                for half in range(2):
                    qh = jnp.where(low, q, 0.0) if half == 0 else jnp.where(low, 0.0, q)
                    chains.append(one_head(qh.astype(BF16), kd, vd, ok, 2 * jj + half))
                where_to.append((rs, jj))
    outs = _interleave(chains)
    for n, (rs, jj) in enumerate(where_to):
        o_ref[rs, jj * LANES:(jj + 1) * LANES] = jnp.where(low, outs[2 * n], outs[2 * n + 1]).astype(BF16)


def sliding_window_attention(proj, sinks, slopes, *, tq=512):
    t = proj.shape[0]
    w = SWA_WINDOW
    qw = SWA_Q_HEADS * SWA_HEAD_DIM
    rpb = tq // w
    qblk = COL_QB // (2 * LANES)
    kblk, vblk = COL_KB // LANES, COL_VB // LANES
    return pl.pallas_call(
        functools.partial(_swa_kernel, tq=tq, slopes=tuple(float(s) for s in slopes)),
        out_shape=jax.ShapeDtypeStruct((t, qw), BF16),
        grid=(t // tq,),
        in_specs=[
            pl.BlockSpec(memory_space=pltpu.SMEM),
            pl.BlockSpec((tq, 2 * LANES), lambda i: (i, qblk)),
            pl.BlockSpec((tq, 2 * LANES), lambda i: (i, qblk + 1)),
            pl.BlockSpec((tq, LANES), lambda i: (i, kblk)),
            pl.BlockSpec((tq, LANES), lambda i: (i, vblk)),
            pl.BlockSpec((w, LANES), lambda i: (jnp.maximum(i * rpb - 1, 0), kblk)),
            pl.BlockSpec((w, LANES), lambda i: (jnp.maximum(i * rpb - 1, 0), vblk)),
        ],
        out_specs=pl.BlockSpec((tq, qw), lambda i: (i, 0)),
        compiler_params=_cparams(("parallel",)),
        name="swa_attention",
    )(sinks, proj, proj, proj, proj, proj, proj)


def _interleave(gens):
    gens = list(gens)
    results = [None] * len(gens)
    live = list(range(len(gens)))
    while live:
        still = []
        for idx in live:
            try:
                next(gens[idx])
                still.append(idx)
            except StopIteration as done:
                results[idx] = done.value
        live = still
    return results


def _seg_sum(x, low):
    s0 = jnp.sum(jnp.where(low, x, 0.0), axis=-1, keepdims=True)
    s1 = jnp.sum(jnp.where(low, 0.0, x), axis=-1, keepdims=True)
    return jnp.where(low, s0, s1)


def _stack_heads(z, low):
    zero = jnp.zeros_like(z)
    return jnp.concatenate([jnp.where(low, z, zero), jnp.where(low, zero, z)], axis=0)


def _round_robin(subs, results):
    live = list(range(len(subs)))
    while live:
        still = []
        for idx in live:
            try:
                next(subs[idx])
                still.append(idx)
            except StopIteration as done:
                results[idx] = done.value
        live = still
        yield


def _rwkv_kernel(*refs, tt, grp):
    nb = grp * LANES // RWKV_BLOCK
    r_refs, k_refs, v_refs = refs[0:nb], refs[nb:2 * nb], refs[2 * nb:3 * nb]
    lo_refs = refs[3 * nb:3 * nb + 4]
    _rwkv_body(r_refs, k_refs, v_refs, lo_refs, *refs[3 * nb + 4:], tt=tt, grp=grp)


def _rwkv_body(r_refs, k_refs, v_refs, lo_refs, pp_ref, w2_ref, a2_ref, g2_ref, y_ref, h_ref, *,
               tt, grp):
    i = pl.program_id(1)
    nch = tt // CHUNK
    c2 = 2 * CHUNK
    gw = grp * LANES

    @pl.when(i == 0)
    def _():
        h_ref[...] = jnp.zeros(h_ref.shape, F32)

    pp = pp_ref[...]
    w0, a0, k_k, k_a, r_k, ln_w, ln_b = (pp[0:1], pp[1:2], pp[2:3], pp[3:4], pp[4:5], pp[5:6],
                                         pp[6:7])
    low_c = lax.broadcasted_iota(jnp.int32, (CHUNK, LANES), 1) < RWKV_HEAD_DIM
    rr = lax.broadcasted_iota(jnp.int32, (CHUNK, LANES), 0)
    cc = lax.broadcasted_iota(jnp.int32, (CHUNK, LANES), 1) & (RWKV_HEAD_DIM - 1)
    lt = rr > cc
    le = rr >= cc
    eye = jnp.where(rr == cc, 1.0, 0.0)
    bd = (lax.broadcasted_iota(jnp.int32, (c2, LANES), 0) < CHUNK) == (
        lax.broadcasted_iota(jnp.int32, (c2, LANES), 1) < RWKV_HEAD_DIM)
    rowc = lax.broadcasted_iota(jnp.int32, (CHUNK, gw), 0)

    def seg_sum(x):
        return jnp.concatenate(
            [_seg_sum(x[:, p * LANES:(p + 1) * LANES], low_c) for p in range(grp)], axis=1)

    def prepare_chain(at, rt, bt, kt, vs, bh, khat, etot):
        sb = _stack_heads(bt, low_c)
        sk = _stack_heads(kt, low_c)
        sv = _stack_heads(vs, low_c)
        a4 = _dot_nt(jnp.concatenate([at, rt], axis=0), jnp.concatenate([sb, sk], axis=0))
        kv = jnp.where(bd, _dot(khat.T.astype(BF16), vs), 0.0)
        bht = bh.T.astype(BF16)
        dx = jnp.broadcast_to(etot, (c2, LANES)).T
        yield
        n = jnp.where(lt, a4[:CHUNK, :LANES], 0.0)
        aak = jnp.where(lt, a4[:CHUNK, LANES:], 0.0)
        arb = jnp.where(le, a4[CHUNK:, :LANES], 0.0)
        ark = jnp.where(le, a4[CHUNK:, LANES:], 0.0)
        av = _dot(jnp.concatenate([aak, ark], axis=0).astype(BF16), sv)
        tinv = eye + n
        pw = n
        for _ in range(int(math.log2(CHUNK)) - 1):
            pb = pw.astype(BF16)
            pw = _dot(pb, _stack_heads(pb, low_c))
            yield
            tinv = tinv + _dot(pw.astype(BF16), _stack_heads(tinv.astype(BF16), low_c))
            yield
        rhs = jnp.concatenate([_stack_heads(at, low_c),
                               _stack_heads(av[:CHUNK].astype(BF16), low_c)], axis=1)
        wu = _dot(tinv.astype(BF16), rhs)
        yield
        return (jnp.concatenate([wu[:, :LANES].astype(BF16), rt], axis=0), wu[:, LANES:],
                av[CHUNK:], arb.astype(BF16), bht, kv, dx)

    def chunk_chain(c):
        rows = slice(c * CHUNK, (c + 1) * CHUNK)
        lo0, lo1, lo2, lo3 = (lo_refs[n][rows, :] for n in range(4))
        z = -(w0 + _dot(jnp.tanh(lo0).astype(BF16), w2_ref[...]))
        a_pre = a0 + _dot(jnp.concatenate([lo0, lo1], axis=1).astype(BF16), a2_ref[...])
        gate_in = jnp.concatenate([lo1, lo2, lo3], axis=1)
        g = _dot((1.0 / (1.0 + jnp.exp(-gate_in))).astype(BF16), g2_ref[...])
        yield
        r, k, v = (jnp.concatenate([ref[rows, :] for ref in refs_], axis=1)
                   for refs_ in (r_refs, k_refs, v_refs))
        softplus = jnp.maximum(z, 0.0) + jnp.log(1.0 + jnp.exp(-jnp.abs(z)))
        lw = -jnp.exp(-softplus - 0.5)
        a = 1.0 / (1.0 + jnp.exp(-a_pre))
        kx = k * k_k
        kk = kx * lax.rsqrt(jnp.maximum(seg_sum(kx * kx), 1e-24))
        k2 = k * (1.0 + (a - 1.0) * k_a)
        bvec = kk * a
        cum = lw
        for s in (1, 2, 4, 8, 16, 32):
            cum = cum + jnp.where(rowc >= s, pltpu.roll(cum, s, axis=0), 0.0)
        tot = cum[CHUNK - 1:CHUNK, :]
        einv = jnp.exp(-cum)
        erem = jnp.exp(tot - cum)
        at = (-kk * jnp.exp(cum - lw)).astype(BF16)
        rt = (r * jnp.exp(cum)).astype(BF16)
        bt = (bvec * einv).astype(BF16)
        kt = (k2 * einv).astype(BF16)
        bh = bvec * erem
        khat = k2 * erem
        vs = v.astype(BF16)
        etot = jnp.exp(tot)
        bonus = seg_sum(r * k2 * r_k) * v
        yield
        pairs = [None] * grp
        subs = []
        for p in range(grp):
            ln = slice(p * LANES, (p + 1) * LANES)
            subs.append(prepare_chain(at[:, ln], rt[:, ln], bt[:, ln], kt[:, ln], vs[:, ln],
                                      bh[:, ln], khat[:, ln], etot[:, ln]))
        yield from _round_robin(subs, pairs)
        return pairs, g, bonus

    hs = [h_ref[p] for p in range(grp)]

    def advance_chain(c, pairs, g, bonus):
        rows = slice(c * CHUNK, (c + 1) * CHUNK)
        whs = [_dot(pairs[p][0], hs[p].astype(BF16)) for p in range(grp)]
        yield
        ys = []
        for p in range(grp):
            _, u0, y0, arb, bht, kv, dx = pairs[p]
            u = (whs[p][:CHUNK] + u0).astype(BF16)
            ys.append(whs[p][CHUNK:] + _dot(arb, _stack_heads(u, low_c)) + y0)
            hs[p] = dx * hs[p] + jnp.where(bd, _dot(bht, u), 0.0) + kv
        yield
        y = jnp.concatenate(ys, axis=1)
        mean = seg_sum(y) * (1.0 / RWKV_HEAD_DIM)
        yc = y - mean
        var = seg_sum(yc * yc) * (1.0 / RWKV_HEAD_DIM)
        yn = yc * lax.rsqrt(var + RWKV_GN_EPS) * ln_w + ln_b
        y_ref[rows, :] = ((yn + bonus) * g).astype(BF16)

    chains = [chunk_chain(c) for c in range(nch)]
    ready = {}
    active, started, rnd = [], 0, 0
    adv, adv_next = None, 0
    while adv_next < nch:
        if started < nch and rnd % CHUNK_STAGGER == 0:
            active.append(started)
            started += 1
        for c in list(active):
            try:
                next(chains[c])
            except StopIteration as done:
                ready[c] = done.value
                active.remove(c)
        if adv is None and adv_next in ready:
            adv = advance_chain(adv_next, *ready.pop(adv_next))
        if adv is not None:
            try:
                next(adv)
            except StopIteration:
                adv, adv_next = None, adv_next + 1
        rnd += 1

    for p in range(grp):
        h_ref[p] = hs[p]


def rwkv7_mix(proj, pp, w2p, a2p, g2p, layer, *, tt=512, grp=RWKV_GROUP):
    t = proj.shape[0]
    c2 = 2 * CHUNK
    gw = grp * LANES
    nb = gw // RWKV_BLOCK

    def col_blocks(col, width, count, per_group):
        return [pl.BlockSpec((tt, width), lambda p, i, o=col // width + n: (i, o + per_group * p))
                for n in range(count)]

    in_specs = (col_blocks(COL_R, RWKV_BLOCK, nb, nb) + col_blocks(COL_K, RWKV_BLOCK, nb, nb)
                + col_blocks(COL_V, RWKV_BLOCK, nb, nb) + col_blocks(COL_LORA, LANES, 4, 0))
    in_specs += [
        pl.BlockSpec((None, SUBLANES, gw), lambda p, i: (layer, 0, p)),
        pl.BlockSpec((None, LANES, gw), lambda p, i: (layer, 0, p)),
        pl.BlockSpec((None, 2 * LANES, gw), lambda p, i: (layer, 0, p)),
        pl.BlockSpec((None, 3 * LANES, gw), lambda p, i: (layer, 0, p)),
    ]
    return pl.pallas_call(
        functools.partial(_rwkv_kernel, tt=tt, grp=grp),
        out_shape=jax.ShapeDtypeStruct((t, RWKV_WIDTH), BF16),
        grid=(RWKV_PAIRS // grp, t // tt),
        in_specs=in_specs,
        out_specs=pl.BlockSpec((tt, gw), lambda p, i: (i, p)),
        scratch_shapes=[pltpu.VMEM((grp, c2, LANES), F32)],
        compiler_params=_cparams(("arbitrary", "arbitrary")),
        name="rwkv7_mix",
    )(*([proj] * (3 * nb + 4)), pp, w2p, a2p, g2p)


def _alibi_slopes():
    idx = np.arange(1, ALIBI_HEADS + 1, dtype=np.float32)
    m = np.exp2(-8.0 * idx / ALIBI_HEADS).astype(np.float32)
    diff_idx = np.arange(2, ALIBI_HEADS, 3)
    swa_idx = np.setdiff1d(np.arange(ALIBI_HEADS), diff_idx)
    return m[diff_idx], m[swa_idx]


def _pad_rows(w, before, total):
    return jnp.pad(w, ((0, 0), (before, total - before - w.shape[1]), (0, 0)))


def kernel(x, attn_norm_g, w_in, diff_lambda, diff_subln_g, swa_sinks, rwkv_mu, rwkv_w0, rwkv_w2,
           rwkv_a0, rwkv_a2, rwkv_g2, rwkv_k_k, rwkv_k_a, rwkv_r_k, rwkv_ln_w, rwkv_ln_b, w_out,
           ffn_norm_g, w_gate_up, w_down, final_norm_g):
    b, t, d = x.shape
    depth = w_in.shape[0]
    diff_slopes, swa_slopes = _alibi_slopes()
    diff_slopes = jnp.asarray(diff_slopes)

    w_in_b = jnp.pad(w_in.astype(BF16), ((0, 0), (0, 0), (0, PROJ_COLS - IN_COLS)))
    w_down_b = w_down.astype(BF16)
    mu_cols = jnp.pad(rwkv_mu, ((0, 0), (IN_COLS - rwkv_mu.shape[1], PROJ_COLS - IN_COLS))).reshape(
        depth, 1, PROJ_COLS)
    w2_b = _pad_rows(rwkv_w2, 0, LANES).astype(BF16)
    a2_b = _pad_rows(rwkv_a2, DECAY_LORA, 2 * LANES).astype(BF16)
    g2_b = _pad_rows(rwkv_g2, DECAY_LORA + ICLR_LORA - LANES, 3 * LANES).astype(BF16)
    pp = jnp.stack([rwkv_w0, rwkv_a0, rwkv_k_k, rwkv_k_a, rwkv_r_k.reshape(depth, RWKV_WIDTH),
                    rwkv_ln_w, rwkv_ln_b, jnp.zeros_like(rwkv_w0)], axis=1)
    attn_g = attn_norm_g.reshape(depth, 1, d)
    ffn_g = ffn_norm_g.reshape(depth, 1, d)

    outs = []
    for bi in range(b):
        xb = x[bi]
        for l in range(depth):
            lambda_init = 0.8 - 0.6 * math.exp(-0.3 * l)
            proj = norm_matmul(xb, attn_g, w_in_b, mu_cols, l)
            ya = diff_attention(proj, diff_lambda[l], diff_subln_g[l], diff_slopes,
                                lambda_init=lambda_init)
            yb = sliding_window_attention(proj, swa_sinks[l], swa_slopes)
            yc = rwkv7_mix(proj, pp, w2_b, a2_b, g2_b, l)
            xb = matmul_residual([ya, yb, yc], w_out, l, xb, tn=1024, name="out_proj")
            hid = ffn_up(xb, ffn_g, w_gate_up, l)
            xb = matmul_residual([hid], w_down_b, l, xb, name="ffn_down")
        outs.append(rmsnorm(xb, final_norm_g))
    return jnp.stack(outs, axis=0)
```
